```python
import math
import jax, jax.numpy as jnp
from jax import lax
import numpy as np

D_MODEL = 1024
BATCH = 2
SEQ = 16384
DEPTH = 1
DEC_BATCH = 16
DEC_SEQ = 64
PAST_LEN = 2048

CHUNK = 64
N_HEADS = 8
QK_NOPE = 64
ROPE_DIM = 32
HEAD_DIM = QK_NOPE + ROPE_DIM
V_DIM = 64
Q_LORA = 384
KV_LORA = 256
ATT_WIDTH = N_HEADS * V_DIM
CONV_CH = D_MODEL // 2
CONV_K = 31
MIX_WIDTH = ATT_WIDTH + CONV_CH
IN_WIDTH = Q_LORA + KV_LORA + ROPE_DIM + 2 * CONV_CH
D_FF = 2816
FFN_K = 3
Q_BLOCK = 128
ROPE_BASE = 10000.0
RMS_EPS = 1e-6
NEG_INF = -1e30
SCALE = HEAD_DIM ** -0.5

kernel_name = "mla_conformer_conv_hybrid_stream_step"


def _rms(x, g):
    xf = x.astype(jnp.float32)
    y = xf * lax.rsqrt(jnp.mean(xf * xf, axis=-1, keepdims=True) + RMS_EPS)
    return (y * g.astype(jnp.float32)).astype(x.dtype)


def _rope_tail(x, pos):
    nope, pe = x[..., :QK_NOPE], x[..., QK_NOPE:]
    inv = 1.0 / (ROPE_BASE ** (jnp.arange(0, ROPE_DIM, 2, dtype=jnp.float32) / ROPE_DIM))
    ang = pos.astype(jnp.float32)[:, None] * inv[None, :]
    cos = jnp.cos(ang)[None, :, None, :].astype(x.dtype)
    sin = jnp.sin(ang)[None, :, None, :].astype(x.dtype)
    p1, p2 = pe[..., :ROPE_DIM // 2], pe[..., ROPE_DIM // 2:]
    return jnp.concatenate([nope, p1 * cos - p2 * sin, p2 * cos + p1 * sin], axis=-1)


def _attend_block(q, k, v, q_pos, k_pos):
    s = jnp.einsum('bqhd,bkhd->bhqk', q, k).astype(jnp.float32) * SCALE
    allowed = (k_pos // CHUNK)[None, :] <= (q_pos // CHUNK)[:, None]
    s = jnp.where(allowed[None, None], s, NEG_INF)
    p = jax.nn.softmax(s, axis=-1).astype(v.dtype)
    return jnp.einsum('bhqk,bkhd->bqhd', p, v)


def _chunk_causal_attention(q, k, v, q_pos, k_pos):
    B, T, H, dh = q.shape
    if T % Q_BLOCK == 0:
        nb = T // Q_BLOCK
        qb = q.reshape(B, nb, Q_BLOCK, H, dh).transpose(1, 0, 2, 3, 4)
        pb = q_pos.reshape(nb, Q_BLOCK)
        out = lax.map(lambda a: _attend_block(a[0], k, v, a[1], k_pos), (qb, pb))
        out = out.transpose(1, 0, 2, 3, 4)
    else:
        out = _attend_block(q, k, v, q_pos, k_pos)
    return out.reshape(B, T, H * V_DIM)


def _causal_dwconv(x_all, w, b):
    C = x_all.shape[-1]
    y = lax.conv_general_dilated(
        x_all, w[:, None, :], window_strides=(1,), padding='VALID',
        dimension_numbers=('NWC', 'WIO', 'NWC'), feature_group_count=C)
    return y + b


def _layer(x, ckv_past, kpe_past, conv_past, ffn_past,
           attn_norm, w_in, q_norm, w_uq, kv_norm, w_ukv, qk_norm_q, qk_norm_k,
           conv_w, conv_b, conv_norm, w_out, ffn_norm, w_up, ffn_conv_w, ffn_conv_b, w_down):
    B, T, _ = x.shape
    pos0 = ckv_past.shape[1]
    h = _rms(x, attn_norm)
    proj = h @ w_in
    c_q = proj[..., :Q_LORA]
    c_kv = _rms(proj[..., Q_LORA:Q_LORA + KV_LORA], kv_norm)
    k_pe = proj[..., Q_LORA + KV_LORA:Q_LORA + KV_LORA + ROPE_DIM]
    glu = proj[..., Q_LORA + KV_LORA + ROPE_DIM:]

    q = (_rms(c_q, q_norm) @ w_uq).reshape(B, T, N_HEADS, HEAD_DIM)
    ckv_all = jnp.concatenate([ckv_past, c_kv], axis=1)
    kpe_all = jnp.concatenate([kpe_past, k_pe], axis=1)
    Tk = ckv_all.shape[1]
    kv = (ckv_all @ w_ukv).reshape(B, Tk, N_HEADS, QK_NOPE + V_DIM)
    k_nope, v = kv[..., :QK_NOPE], kv[..., QK_NOPE:]
    k = jnp.concatenate(
        [k_nope, jnp.broadcast_to(kpe_all[:, :, None, :], (B, Tk, N_HEADS, ROPE_DIM))], axis=-1)
    q_pos = pos0 + jnp.arange(T, dtype=jnp.int32)
    k_pos = jnp.arange(Tk, dtype=jnp.int32)
    q = _rope_tail(_rms(q, qk_norm_q), q_pos)
    k = _rope_tail(_rms(k, qk_norm_k), k_pos)
    att = _chunk_causal_attention(q, k, v, q_pos, k_pos)

    u = glu[..., :CONV_CH] * jax.nn.sigmoid(glu[..., CONV_CH:])
    u_all = jnp.concatenate([conv_past, u], axis=1)
    c = jax.nn.silu(_rms(_causal_dwconv(u_all, conv_w, conv_b), conv_norm))

    x = x + jnp.concatenate([att, c], axis=-1) @ w_out

    up = _rms(x, ffn_norm) @ w_up
    a, gate = up[..., :D_FF], up[..., D_FF:]
    a_all = jnp.concatenate([ffn_past, a], axis=1)
    a = _causal_dwconv(a_all, ffn_conv_w, ffn_conv_b)
    y = x + (jax.nn.silu(a) * gate) @ w_down

    return y, c_kv, k_pe, u_all[:, -(CONV_K - 1):], a_all[:, -(FFN_K - 1):]


def setup_inputs(seed: int = 0) -> dict:
    key = jax.random.key(seed)
    ks = jax.random.split(key, 24)
    f32 = jnp.float32
    nrm = lambda k, shape, s: jax.random.normal(k, shape, f32) * s
    gain = lambda k, n: 1.0 + 0.01 * jax.random.normal(k, (DEPTH, n), f32)
    return {
        "x_prompt": nrm(ks[0], (BATCH, SEQ, D_MODEL), 1.0),
        "x_sample": nrm(ks[1], (DEC_BATCH, DEC_SEQ, D_MODEL), 1.0),
        "cache_ckv": nrm(ks[2], (DEPTH, DEC_BATCH, PAST_LEN, KV_LORA), 1.0),
        "cache_kpe": nrm(ks[3], (DEPTH, DEC_BATCH, PAST_LEN, ROPE_DIM), 0.5),
        "state_conv": nrm(ks[4], (DEPTH, DEC_BATCH, CONV_K - 1, CONV_CH), 0.5),
        "state_ffn_conv": nrm(ks[5], (DEPTH, DEC_BATCH, FFN_K - 1, D_FF), 0.5),
        "attn_norm": gain(ks[6], D_MODEL),
        "w_in": nrm(ks[7], (DEPTH, D_MODEL, IN_WIDTH), D_MODEL ** -0.5),
        "q_norm": gain(ks[8], Q_LORA),
        "w_uq": nrm(ks[9], (DEPTH, Q_LORA, N_HEADS * HEAD_DIM), Q_LORA ** -0.5),
        "kv_norm": gain(ks[10], KV_LORA),
        "w_ukv": nrm(ks[11], (DEPTH, KV_LORA, N_HEADS * (QK_NOPE + V_DIM)), KV_LORA ** -0.5),
        "qk_norm_q": gain(ks[12], HEAD_DIM),
        "qk_norm_k": gain(ks[13], HEAD_DIM),
        "conv_w": nrm(ks[14], (DEPTH, CONV_K, CONV_CH), CONV_K ** -0.5),
        "conv_b": nrm(ks[15], (DEPTH, CONV_CH), 0.01),
        "conv_norm": gain(ks[16], CONV_CH),
        "w_out": nrm(ks[17], (DEPTH, MIX_WIDTH, D_MODEL), MIX_WIDTH ** -0.5),
        "ffn_norm": gain(ks[18], D_MODEL),
        "w_up": nrm(ks[19], (DEPTH, D_MODEL, 2 * D_FF), D_MODEL ** -0.5),
        "ffn_conv_w": nrm(ks[20], (DEPTH, FFN_K, D_FF), FFN_K ** -0.5),
        "ffn_conv_b": nrm(ks[21], (DEPTH, D_FF), 0.01),
        "w_down": nrm(ks[22], (DEPTH, D_FF, D_MODEL), D_FF ** -0.5),
    }


def reference(x_prompt, x_sample, cache_ckv, cache_kpe, state_conv, state_ffn_conv,
              attn_norm, w_in, q_norm, w_uq, kv_norm, w_ukv, qk_norm_q, qk_norm_k,
              conv_w, conv_b, conv_norm, w_out, ffn_norm, w_up, ffn_conv_w, ffn_conv_b, w_down):
    B = x_prompt.shape[0]
    dt = x_prompt.dtype
    yp, ys = x_prompt, x_sample
    p_ckv, p_kpe, p_conv, p_ffn = [], [], [], []
    s_ckv, s_kpe, s_conv, s_ffn = [], [], [], []
    for l in range(DEPTH):
        w = (attn_norm[l], w_in[l], q_norm[l], w_uq[l], kv_norm[l], w_ukv[l], qk_norm_q[l],
             qk_norm_k[l], conv_w[l], conv_b[l], conv_norm[l], w_out[l], ffn_norm[l], w_up[l],
             ffn_conv_w[l], ffn_conv_b[l], w_down[l])
        yp, c1, k1, cv1, f1 = _layer(
            yp, jnp.zeros((B, 0, KV_LORA), dt), jnp.zeros((B, 0, ROPE_DIM), dt),
            jnp.zeros((B, CONV_K - 1, CONV_CH), dt), jnp.zeros((B, FFN_K - 1, D_FF), dt), *w)
        ys, c2, k2, cv2, f2 = _layer(
            ys, cache_ckv[l], cache_kpe[l], state_conv[l], state_ffn_conv[l], *w)
        p_ckv.append(c1); p_kpe.append(k1); p_conv.append(cv1); p_ffn.append(f1)
        s_ckv.append(c2); s_kpe.append(k2); s_conv.append(cv2); s_ffn.append(f2)
    return (yp, ys,
            jnp.stack(p_ckv), jnp.stack(p_kpe), jnp.stack(p_conv), jnp.stack(p_ffn),
            jnp.stack(s_ckv), jnp.stack(s_kpe), jnp.stack(s_conv), jnp.stack(s_ffn))
```

```python
import functools

import jax
import jax.numpy as jnp
from jax import lax
from jax.experimental import pallas as pl
from jax.experimental.pallas import tpu as pltpu

CHUNK = 64
N_HEADS = 8
QK_NOPE = 64
ROPE_DIM = 32
HEAD_DIM = QK_NOPE + ROPE_DIM
V_DIM = 64
Q_LORA = 384
KV_LORA = 256
CONV_K = 31
FFN_K = 3
ROPE_BASE = 10000.0
RMS_EPS = 1e-6
NEG_INF = -1e30
SCALE = HEAD_DIM ** -0.5

LANES = 128
HEAD_PAIRS = N_HEADS // 2
CONV_HDR = 32
FFN_HDR = 8
VMEM_LIMIT = 56 * 1024 * 1024

F32 = jnp.float32
BF16 = jnp.bfloat16


def _rms(x, g):
    return x * lax.rsqrt(jnp.mean(x * x, axis=-1, keepdims=True) + RMS_EPS) * g


def _dot(a, b):
    return jnp.dot(a, b, preferred_element_type=F32)


def _rope_tables(pos_f32, inv_ref, ga_ref, gb_ref):
    ang = pos_f32 * inv_ref[...]
    return ga_ref[...] * jnp.cos(ang), gb_ref[...] * jnp.sin(ang)


def _params(sem):
    return pltpu.CompilerParams(dimension_semantics=sem, vmem_limit_bytes=VMEM_LIMIT)


def _const_spec(shape):
    nd = len(shape)
    return pl.BlockSpec(shape, lambda *_: (0,) * nd)


def _pre_kernel(x_ref, cst_ref, an_ref, win_ref, qn_ref, wuq_ref, kvn_ref, inv_ref, ga_ref, gb_ref,
                cw_ref, cb_ref, cn_ref,
                ckv_ref, kpe_ref, kpe2_ref, q_ref, c_ref, cstate_ref, ext_ref, *, S, L, pos0, RC):
    t = pl.program_id(1)
    conv_ch = cw_ref.shape[1]

    x = x_ref[...]
    h = _rms(x, an_ref[...]).astype(BF16)

    c_q = _dot(h, win_ref[:, :Q_LORA])
    hq = _rms(c_q, qn_ref[...]).astype(BF16)
    c_kv = _dot(h, win_ref[:, Q_LORA:Q_LORA + KV_LORA])
    ckv_ref[...] = _rms(c_kv, kvn_ref[...])
    o = Q_LORA + KV_LORA
    pe3 = _dot(h, win_ref[:, o:o + 3 * LANES])
    kpe_ref[...] = pe3[:, :ROPE_DIM]
    kpe2_ref[...] = pe3[:, LANES:]
    o += 3 * LANES
    glu_a = _dot(h, win_ref[:, o:o + conv_ch])
    glu_b = _dot(h, win_ref[:, o + conv_ch:o + 2 * conv_ch])
    u = glu_a * jax.nn.sigmoid(glu_b)

    qa = _dot(hq, wuq_ref[:, :N_HEADS * LANES])
    qb = _dot(hq, wuq_ref[:, N_HEADS * LANES:])
    pos = (pos0 + t * L + lax.broadcasted_iota(jnp.int32, (L, 1), 0)).astype(F32)
    ta, tb = _rope_tables(pos, inv_ref, ga_ref, gb_ref)
    for s in range(S):
        rows = slice(s * L, (s + 1) * L)
        for hd in range(N_HEADS):
            cols = slice(hd * LANES, (hd + 1) * LANES)
            qh = qa[rows, cols]
            r = lax.rsqrt(jnp.sum(qh * qh, axis=-1, keepdims=True) * (1.0 / HEAD_DIM) + RMS_EPS) * SCALE
            q_ref[rows, cols] = ((qh * ta + qb[rows, cols] * tb) * r).astype(BF16)

    @pl.when(t == 0)
    def _():
        ext_ref[:, :CONV_HDR, :] = cst_ref[...]

    @pl.when(t > 0)
    def _():
        ext_ref[:, :CONV_HDR, :] = ext_ref[:, L:L + CONV_HDR, :]

    for s in range(S):
        ext_ref[s, CONV_HDR:, :] = u[s * L:(s + 1) * L]
    cstate_ref[...] = ext_ref[:, L:L + CONV_HDR, :]

    first = CONV_HDR - (CONV_K - 1)
    for s in range(S):
        for r0 in range(0, L, RC):
            acc = jnp.broadcast_to(cb_ref[...], (RC, conv_ch))
            for k in range(CONV_K):
                acc = acc + cw_ref[k:k + 1, :] * ext_ref[s, first + k + r0:first + k + r0 + RC, :]
            y = _rms(acc, cn_ref[...])
            c_ref[s * L + r0:s * L + r0 + RC, :] = (y * jax.nn.sigmoid(y)).astype(BF16)


def _pre_call(x2d, cst, an, win, qn, wuq, kvn, inv, ga, gb, cw, cb, cn, *, nb, nt, S, L, pos0):
    rows, d = x2d.shape
    tm = S * L
    conv_ch = cw.shape[1]
    row_blk = lambda w: pl.BlockSpec((tm, w), lambda b, t: (b * nt + t, 0))
    out_shape = (
        jax.ShapeDtypeStruct((rows, KV_LORA), F32),
        jax.ShapeDtypeStruct((rows, ROPE_DIM), F32),
        jax.ShapeDtypeStruct((rows, 2 * LANES), F32),
        jax.ShapeDtypeStruct((rows, N_HEADS * LANES), BF16),
        jax.ShapeDtypeStruct((rows, conv_ch), BF16),
        jax.ShapeDtypeStruct((nb * S, CONV_HDR, conv_ch), F32),
    )
    state_blk = pl.BlockSpec((S, CONV_HDR, conv_ch), lambda b, t: (b, 0, 0))
    return pl.pallas_call(
        functools.partial(_pre_kernel, S=S, L=L, pos0=pos0, RC=min(L, 64)),
        grid=(nb, nt),
        in_specs=[row_blk(d), state_blk] + [_const_spec(a.shape) for a in (an, win, qn, wuq, kvn, inv, ga, gb, cw, cb, cn)],
        out_specs=(row_blk(KV_LORA), row_blk(ROPE_DIM), row_blk(2 * LANES), row_blk(N_HEADS * LANES),
                   row_blk(conv_ch), state_blk),
        out_shape=out_shape,
        scratch_shapes=[pltpu.VMEM((S, CONV_HDR + L, conv_ch), F32)],
        compiler_params=_params(("arbitrary", "arbitrary")),
        name="pre",
    )(x2d, cst, an, win, qn, wuq, kvn, inv, ga, gb, cw, cb, cn)


def _kvup_kernel(ckv_ref, kpe2_ref, wk_ref, wv_ref, inv_ref, ga_ref, gb_ref, k_ref, v_ref, *, L):
    t = pl.program_id(1)
    ckv = ckv_ref[...].astype(BF16)
    v_ref[...] = _dot(ckv, wv_ref[...]).astype(BF16)
    kn = _dot(ckv, wk_ref[...])
    pe = kpe2_ref[:, :LANES]
    pe_swapped = kpe2_ref[:, LANES:]
    pos = (t * L + lax.broadcasted_iota(jnp.int32, (L, 1), 0)).astype(F32)
    ta, tb = _rope_tables(pos, inv_ref, ga_ref, gb_ref)
    rot = pe_swapped * tb
    for hd in range(N_HEADS):
        cols = slice(hd * LANES, (hd + 1) * LANES)
        kh = kn[:, cols] + pe
        r = lax.rsqrt(jnp.sum(kh * kh, axis=-1, keepdims=True) * (1.0 / HEAD_DIM) + RMS_EPS)
        k_ref[:, cols] = ((kh * ta + rot) * r).astype(BF16)


def _kvup_call(ckv2d, kpe2, wk, wv, inv, ga, gb, *, nb, nt, L):
    rows = ckv2d.shape[0]
    row_blk = lambda w: pl.BlockSpec((L, w), lambda b, t: (b * nt + t, 0))
    return pl.pallas_call(
        functools.partial(_kvup_kernel, L=L),
        grid=(nb, nt),
        in_specs=[row_blk(KV_LORA), row_blk(2 * LANES)] + [_const_spec(a.shape) for a in (wk, wv, inv, ga, gb)],
        out_specs=(row_blk(N_HEADS * LANES), row_blk(N_HEADS * V_DIM)),
        out_shape=(jax.ShapeDtypeStruct((rows, N_HEADS * LANES), BF16),
                   jax.ShapeDtypeStruct((rows, N_HEADS * V_DIM), BF16)),
        compiler_params=_params(("arbitrary", "arbitrary")),
        name="kvup",
    )(ckv2d, kpe2, wk, wv, inv, ga, gb)


def _qk(q, k):
    return lax.dot_general(q, k, (((1,), (1,)), ((), ())), preferred_element_type=F32)


def _attn_prompt_kernel(q_ref, k_ref, v_ref, o_ref, m_ref, l_ref, acc_ref, *, blk):
    qi = pl.program_id(2)
    row_chunk = lax.broadcasted_iota(jnp.int32, (blk, blk), 0) // CHUNK
    col_chunk = lax.broadcasted_iota(jnp.int32, (blk, blk), 1) // CHUNK
    diag_allowed = col_chunk <= row_chunk
    lane = lax.broadcasted_iota(jnp.int32, (blk, LANES), 1)

    out = None
    for hh in range(2):
        cols = slice(hh * LANES, (hh + 1) * LANES)
        q = q_ref[:, cols]
        m_ref[...] = jnp.full((blk, 1), NEG_INF, F32)
        l_ref[...] = jnp.zeros((blk, 1), F32)
        acc_ref[...] = jnp.zeros((blk, LANES), F32)

        def step(j, masked):
            start = pl.multiple_of(j * blk, blk)
            s = _qk(q, k_ref[pl.ds(start, blk), cols])
            if masked:
                s = jnp.where(diag_allowed, s, NEG_INF)
            m_old = m_ref[...]
            m_new = jnp.maximum(m_old, jnp.max(s, axis=-1, keepdims=True))
            p = jnp.exp(s - m_new)
            alpha = jnp.exp(m_old - m_new)
            l_ref[...] = alpha * l_ref[...] + jnp.sum(p, axis=-1, keepdims=True)
            acc_ref[...] = alpha * acc_ref[...] + _dot(p.astype(BF16), v_ref[pl.ds(start, blk), :])
            m_ref[...] = m_new

        def body(j, carry):
            step(j, False)
            return carry

        lax.fori_loop(0, qi, body, 0)
        step(qi, True)
        res = acc_ref[...] / l_ref[...]
        out = res if out is None else jnp.where(lane < V_DIM, out, res)
    o_ref[...] = out.astype(BF16)


def _attn_prompt_call(q2d, k2d, v2d, *, B, T, blk):
    nq = T // blk
    return pl.pallas_call(
        functools.partial(_attn_prompt_kernel, blk=blk),
        grid=(B, HEAD_PAIRS, nq),
        in_specs=[pl.BlockSpec((blk, 2 * LANES), lambda b, hp, qi: (b * nq + qi, hp)),
                  pl.BlockSpec((T, 2 * LANES), lambda b, hp, qi: (b, hp)),
                  pl.BlockSpec((T, 2 * V_DIM), lambda b, hp, qi: (b, hp))],
        out_specs=pl.BlockSpec((blk, 2 * V_DIM), lambda b, hp, qi: (b * nq + qi, hp)),
        out_shape=jax.ShapeDtypeStruct((B * T, N_HEADS * V_DIM), BF16),
        scratch_shapes=[pltpu.VMEM((blk, 1), F32), pltpu.VMEM((blk, 1), F32), pltpu.VMEM((blk, LANES), F32)],
        compiler_params=_params(("arbitrary", "arbitrary", "arbitrary")),
        name="attn_prompt",
    )(q2d, k2d, v2d)


def _attn_block_kernel(q_ref, k_ref, v_ref, o_ref, *, T, Tk, pos0):
    q_chunk = (pos0 + lax.broadcasted_iota(jnp.int32, (T, Tk), 0)) // CHUNK
    k_chunk = lax.broadcasted_iota(jnp.int32, (T, Tk), 1) // CHUNK
    allowed = k_chunk <= q_chunk
    lane = lax.broadcasted_iota(jnp.int32, (T, LANES), 1)
    out = None
    for hh in range(2):
        cols = slice(hh * LANES, (hh + 1) * LANES)
        s = jnp.where(allowed, _qk(q_ref[:, cols], k_ref[:, cols]), NEG_INF)
        p = jnp.exp(s - jnp.max(s, axis=-1, keepdims=True))
        l = jnp.sum(p, axis=-1, keepdims=True)
        res = _dot(p.astype(BF16), v_ref[...]) / l
        out = res if out is None else jnp.where(lane < V_DIM, out, res)
    o_ref[...] = out.astype(BF16)


def _attn_block_call(q2d, k2d, v2d, *, B, T, Tk, pos0):
    return pl.pallas_call(
        functools.partial(_attn_block_kernel, T=T, Tk=Tk, pos0=pos0),
        grid=(B, HEAD_PAIRS),
        in_specs=[pl.BlockSpec((T, 2 * LANES), lambda b, hp: (b, hp)),
                  pl.BlockSpec((Tk, 2 * LANES), lambda b, hp: (b, hp)),
                  pl.BlockSpec((Tk, 2 * V_DIM), lambda b, hp: (b, hp))],
        out_specs=pl.BlockSpec((T, 2 * V_DIM), lambda b, hp: (b, hp)),
        out_shape=jax.ShapeDtypeStruct((B * T, N_HEADS * V_DIM), BF16),
        compiler_params=_params(("arbitrary", "arbitrary")),
        name="attn_block",
    )(q2d, k2d, v2d)


def _post_kernel(x_ref, att_ref, c_ref, fst_ref, wout_ref, fn_ref, wup_ref, fw_ref, fb_ref, wdn_ref,
                 y_ref, fstate_ref, ext_ref, *, S, L, FC):
    t = pl.program_id(1)
    att_w = att_ref.shape[1]
    d_ff = wdn_ref.shape[0]

    @pl.when(t == 0)
    def _():
        fstate_ref[...] = fst_ref[...]

    x2 = x_ref[...] + _dot(att_ref[...], wout_ref[:att_w, :]) + _dot(c_ref[...], wout_ref[att_w:, :])
    h2 = _rms(x2, fn_ref[...]).astype(BF16)
    y_ref[...] = x2
    for j in range(d_ff // FC):
        cols = slice(j * FC, (j + 1) * FC)
        a = _dot(h2, wup_ref[:, cols])
        gate = _dot(h2, wup_ref[:, d_ff + j * FC:d_ff + (j + 1) * FC])
        for s in range(S):
            ext_ref[s, :FFN_HDR, :] = fstate_ref[s, :, cols]
            ext_ref[s, FFN_HDR:, :] = a[s * L:(s + 1) * L]
            fstate_ref[s, :, cols] = a[(s + 1) * L - FFN_HDR:(s + 1) * L]
        acts = []
        for s in range(S):
            conv = fb_ref[:, cols] + fw_ref[2:3, cols] * a[s * L:(s + 1) * L]
            for k in range(FFN_K - 1):
                off = FFN_HDR - (FFN_K - 1) + k
                conv = conv + fw_ref[k:k + 1, cols] * ext_ref[s, off:off + L, :]
            acts.append((conv * jax.nn.sigmoid(conv) * gate[s * L:(s + 1) * L]).astype(BF16))
        act = acts[0] if S == 1 else jnp.concatenate(acts, axis=0)
        y_ref[...] += _dot(act, wdn_ref[cols, :])


def _post_call(x2d, att, c, fst, wout, fn, wup, fw, fb, wdn, *, nb, nt, S, L, FC):
    rows, d = x2d.shape
    tm = S * L
    d_ff = wdn.shape[0]
    row_blk = lambda w: pl.BlockSpec((tm, w), lambda b, t: (b * nt + t, 0))
    state_blk = pl.BlockSpec((S, FFN_HDR, d_ff), lambda b, t: (b, 0, 0))
    return pl.pallas_call(
        functools.partial(_post_kernel, S=S, L=L, FC=FC),
        grid=(nb, nt),
        in_specs=[row_blk(d), row_blk(att.shape[1]), row_blk(c.shape[1]), state_blk]
        + [_const_spec(a.shape) for a in (wout, fn, wup, fw, fb, wdn)],
        out_specs=(row_blk(d), state_blk),
        out_shape=(jax.ShapeDtypeStruct((rows, d), F32), jax.ShapeDtypeStruct((nb * S, FFN_HDR, d_ff), F32)),
        scratch_shapes=[pltpu.VMEM((S, FFN_HDR + L, FC), F32)],
        compiler_params=_params(("arbitrary", "arbitrary")),
        name="post",
    )(x2d, att, c, fst, wout, fn, wup, fw, fb, wdn)


def _head_groups(w, n_rows):
    w = w.reshape(n_rows, N_HEADS, HEAD_DIM)
    nope, pe = w[..., :QK_NOPE], w[..., QK_NOPE:]
    pe_sw = jnp.concatenate([pe[..., ROPE_DIM // 2:], pe[..., :ROPE_DIM // 2]], axis=-1)
    pad = jnp.zeros((n_rows, N_HEADS, LANES - HEAD_DIM), w.dtype)
    straight = jnp.concatenate([nope, pe, pad], axis=-1).reshape(n_rows, N_HEADS * LANES)
    swapped = jnp.concatenate([jnp.zeros_like(nope), pe_sw, pad], axis=-1).reshape(n_rows, N_HEADS * LANES)
    return straight, swapped


def _rope_lane_consts(g):
    half = ROPE_DIM // 2
    inv = 1.0 / (ROPE_BASE ** (jnp.arange(0, ROPE_DIM, 2, dtype=F32) / ROPE_DIM))
    z = lambda n: jnp.zeros((n,), F32)
    inv_lane = jnp.concatenate([z(QK_NOPE), inv, inv, z(LANES - HEAD_DIM)])
    ga = jnp.concatenate([g, z(LANES - HEAD_DIM)])
    gb = jnp.concatenate([z(QK_NOPE), -g[QK_NOPE + half:], g[QK_NOPE:QK_NOPE + half], z(LANES - HEAD_DIM)])
    return inv_lane[None], ga[None], gb[None]


def _place_pe(pe):
    half = ROPE_DIM // 2
    z = lambda n: jnp.zeros(pe.shape[:-1] + (n,), pe.dtype)
    pe_sw = jnp.concatenate([pe[..., half:], pe[..., :half]], axis=-1)
    return jnp.concatenate([z(QK_NOPE), pe, z(LANES - HEAD_DIM), z(QK_NOPE), pe_sw, z(LANES - HEAD_DIM)], axis=-1)


def _prep_weights(w_in, w_uq, w_ukv, w_out, w_up, w_down):
    d = w_in.shape[0]
    o = Q_LORA + KV_LORA
    w_pe = w_in[:, o:o + ROPE_DIM]
    pe_out = jnp.concatenate([w_pe, jnp.zeros((d, LANES - ROPE_DIM), w_in.dtype)], axis=-1)
    win = jnp.concatenate([w_in[:, :o], pe_out, _place_pe(w_pe), w_in[:, o + ROPE_DIM:]], axis=-1).astype(BF16)
    wuq = jnp.concatenate(_head_groups(w_uq, Q_LORA), axis=-1).astype(BF16)
    wkv = w_ukv.reshape(KV_LORA, N_HEADS, QK_NOPE + V_DIM)
    wk = jnp.concatenate([wkv[..., :QK_NOPE], jnp.zeros((KV_LORA, N_HEADS, LANES - QK_NOPE), w_ukv.dtype)], axis=-1)
    wk = wk.reshape(KV_LORA, N_HEADS * LANES).astype(BF16)
    wv = wkv[..., QK_NOPE:].reshape(KV_LORA, N_HEADS * V_DIM).astype(BF16)
    return win, wuq, wk, wv, w_out.astype(BF16), w_up.astype(BF16), w_down.astype(BF16)


def _pad_front(a, n):
    return jnp.pad(a, ((0, 0), (n - a.shape[1], 0), (0, 0)))


def _layer(x, ckv_past, kpe_past, conv_past, ffn_past, w, *, tile_rows, kv_rows, attn_blk, seg_per_tile, ffn_chunk):
    (attn_norm, w_in, q_norm, w_uq, kv_norm, w_ukv, qk_norm_q, qk_norm_k, conv_w, conv_b, conv_norm,
     w_out, ffn_norm, w_up, ffn_conv_w, ffn_conv_b, w_down) = w
    B, T, D = x.shape
    pos0 = ckv_past.shape[1]
    Tk = pos0 + T
    win, wuq, wk, wv, wout, wup, wdn = _prep_weights(w_in, w_uq, w_ukv, w_out, w_up, w_down)
    inv_q, ga_q, gb_q = _rope_lane_consts(qk_norm_q)
    inv_k, ga_k, gb_k = _rope_lane_consts(qk_norm_k)
    row = lambda v: v[None].astype(F32)

    if T >= tile_rows:
        S, L, nb, nt = 1, tile_rows, B, T // tile_rows
    else:
        S, L, nb, nt = seg_per_tile, T, B // seg_per_tile, 1

    x2d = x.reshape(B * T, D)
    ckv, kpe, kpe2, q, c, cstate = _pre_call(
        x2d, _pad_front(conv_past, CONV_HDR), row(attn_norm), win, row(q_norm), wuq, row(kv_norm),
        inv_q, ga_q, gb_q, conv_w, row(conv_b), row(conv_norm), nb=nb, nt=nt, S=S, L=L, pos0=pos0)

    ckv_all = jnp.concatenate([ckv_past, ckv.reshape(B, T, KV_LORA)], axis=1).reshape(B * Tk, KV_LORA)
    kpe2_all = jnp.concatenate([_place_pe(kpe_past), kpe2.reshape(B, T, 2 * LANES)], axis=1).reshape(B * Tk, 2 * LANES)
    Lk = max(n for n in range(16, kv_rows + 1, 16) if Tk % n == 0)
    k, v = _kvup_call(ckv_all, kpe2_all, wk, wv, inv_k, ga_k, gb_k, nb=B, nt=Tk // Lk, L=Lk)

    if pos0 == 0 and T % attn_blk == 0:
        att = _attn_prompt_call(q, k, v, B=B, T=T, blk=attn_blk)
    else:
        att = _attn_block_call(q, k, v, B=B, T=T, Tk=Tk, pos0=pos0)

    y, fstate = _post_call(x2d, att, c, _pad_front(ffn_past, FFN_HDR), wout, row(ffn_norm), wup,
                           ffn_conv_w, row(ffn_conv_b), wdn, nb=nb, nt=nt, S=S, L=L, FC=ffn_chunk)
    return (y.reshape(B, T, D), ckv.reshape(B, T, KV_LORA), kpe.reshape(B, T, ROPE_DIM),
            cstate[:, CONV_HDR - (CONV_K - 1):], fstate[:, FFN_HDR - (FFN_K - 1):])


def kernel(x_prompt, x_sample, cache_ckv, cache_kpe, state_conv, state_ffn_conv, attn_norm, w_in, q_norm, w_uq,
           kv_norm, w_ukv, qk_norm_q, qk_norm_k, conv_w, conv_b, conv_norm, w_out, ffn_norm, w_up, ffn_conv_w,
           ffn_conv_b, w_down):
    depth = w_in.shape[0]
    B = x_prompt.shape[0]
    dt = x_prompt.dtype
    d_ff = w_down.shape[1]
    conv_ch = conv_w.shape[2]
    cfg = dict(tile_rows=512, kv_rows=1024, attn_blk=512, seg_per_tile=4, ffn_chunk=256)
    yp, ys = x_prompt, x_sample
    outs_p, outs_s = [], []
    for l in range(depth):
        w = (attn_norm[l], w_in[l], q_norm[l], w_uq[l], kv_norm[l], w_ukv[l], qk_norm_q[l], qk_norm_k[l],
             conv_w[l], conv_b[l], conv_norm[l], w_out[l], ffn_norm[l], w_up[l], ffn_conv_w[l], ffn_conv_b[l],
             w_down[l])
        yp, *rest_p = _layer(yp, jnp.zeros((B, 0, KV_LORA), dt), jnp.zeros((B, 0, ROPE_DIM), dt),
                             jnp.zeros((B, CONV_K - 1, conv_ch), dt), jnp.zeros((B, FFN_K - 1, d_ff), dt), w, **cfg)
        ys, *rest_s = _layer(ys, cache_ckv[l], cache_kpe[l], state_conv[l], state_ffn_conv[l], w, **cfg)
        outs_p.append(rest_p)
        outs_s.append(rest_s)
    stack = lambda outs, i: jnp.stack([o[i] for o in outs])
    return (yp, ys, stack(outs_p, 0), stack(outs_p, 1), stack(outs_p, 2), stack(outs_p, 3),
            stack(outs_s, 0), stack(outs_s, 1), stack(outs_s, 2), stack(outs_s, 3))
```

```python
import functools

import jax
import jax.numpy as jnp
from jax import lax
from jax.experimental import pallas as pl
from jax.experimental.pallas import tpu as pltpu

CHUNK = 64
N_HEADS = 8
QK_NOPE = 64
ROPE_DIM = 32
HEAD_DIM = QK_NOPE + ROPE_DIM
V_DIM = 64
Q_LORA = 384
KV_LORA = 256
CONV_K = 31
FFN_K = 3
ROPE_BASE = 10000.0
RMS_EPS = 1e-6
NEG_INF = -1e30
SCALE = HEAD_DIM ** -0.5
Q_SCALE = SCALE * 1.4426950408889634

LANES = 128
SUBLANES = 8
HEAD_PAIRS = N_HEADS // 2
V_ROWS = 80
CONV_HDR = 32
FFN_HDR = 8
VMEM_LIMIT = 56 * 1024 * 1024

F32 = jnp.float32
BF16 = jnp.bfloat16


def _rms(x, g):
    return x * lax.rsqrt(jnp.mean(x * x, axis=-1, keepdims=True) + RMS_EPS) * g


def _dot(a, b):
    return jnp.dot(a, b, preferred_element_type=F32)


def _rope_tables(pos_f32, inv_ref, ga_ref, gb_ref):
    ang = pos_f32 * inv_ref[...]
    return ga_ref[...] * jnp.cos(ang), gb_ref[...] * jnp.sin(ang)


def _params(sem):
    return pltpu.CompilerParams(dimension_semantics=sem, vmem_limit_bytes=VMEM_LIMIT)


def _const_spec(shape):
    nd = len(shape)
    return pl.BlockSpec(shape, lambda *_: (0,) * nd)


def _pre_kernel(x_ref, cst_ref, an_ref, win_ref, qn_ref, wuq_ref, kvn_ref, inv_ref, ga_ref, gb_ref,
                cw_ref, cb_ref, cn_ref,
                ckv_ref, kpe_ref, kpe2_ref, q_ref, c_ref, cstate_ref, ext_ref, conv_ref, *, S, L, pos0, RC, CB,
                q_transposed):
    t = pl.program_id(1)
    conv_ch = cw_ref.shape[1]

    x = x_ref[...]
    h = _rms(x, an_ref[...]).astype(BF16)

    c_q = _dot(h, win_ref[:, :Q_LORA])
    hq = _rms(c_q, qn_ref[...]).astype(BF16)
    c_kv = _dot(h, win_ref[:, Q_LORA:Q_LORA + KV_LORA])
    ckv_ref[...] = _rms(c_kv, kvn_ref[...])
    o = Q_LORA + KV_LORA
    pe3 = _dot(h, win_ref[:, o:o + 3 * LANES])
    kpe_ref[...] = pe3[:, :ROPE_DIM]
    kpe2_ref[...] = pe3[:, LANES:]
    o += 3 * LANES
    glu_a = _dot(h, win_ref[:, o:o + conv_ch])
    glu_b = _dot(h, win_ref[:, o + conv_ch:o + 2 * conv_ch])
    u = glu_a * jax.nn.sigmoid(glu_b)

    qa = _dot(hq, wuq_ref[:, :N_HEADS * LANES])
    qb = _dot(hq, wuq_ref[:, N_HEADS * LANES:])
    pos = (pos0 + t * L + lax.broadcasted_iota(jnp.int32, (L, 1), 0)).astype(F32)
    ta, tb = _rope_tables(pos, inv_ref, ga_ref, gb_ref)
    for s in range(S):
        rows = slice(s * L, (s + 1) * L)
        for hd in range(N_HEADS):
            cols = slice(hd * LANES, (hd + 1) * LANES)
            qh = qa[rows, cols]
            r = lax.rsqrt(jnp.sum(qh * qh, axis=-1, keepdims=True) * (1.0 / HEAD_DIM) + RMS_EPS) * Q_SCALE
            qr = (qh * ta + qb[rows, cols] * tb) * r
            if q_transposed:
                q_ref[cols, rows] = qr.T.astype(BF16)
            else:
                q_ref[rows, cols] = qr.astype(BF16)

    @pl.when(t == 0)
    def _():
        ext_ref[:, :CONV_HDR, :] = cst_ref[...]

    @pl.when(t > 0)
    def _():
        ext_ref[:, :CONV_HDR, :] = ext_ref[:, L:L + CONV_HDR, :]

    for s in range(S):
        ext_ref[s, CONV_HDR:, :] = u[s * L:(s + 1) * L]
    cstate_ref[...] = ext_ref[:, L:L + CONV_HDR, :]

    first = CONV_HDR - (CONV_K - 1)
    for s in range(S):
        for r0 in range(0, L, RC):
            for c0 in range(0, conv_ch, CB):
                ch = slice(c0, c0 + CB)
                y = jnp.broadcast_to(cb_ref[:, ch], (RC, CB))
                for res in range(SUBLANES):
                    taps = [k for k in range(CONV_K) if (first + k) % SUBLANES == res]
                    n_rows = RC + (SUBLANES if res else 0)
                    z = None
                    for k in taps:
                        a0 = r0 + first + k - res
                        term = cw_ref[k:k + 1, ch] * ext_ref[s, a0:a0 + n_rows, ch]
                        z = term if z is None else z + term
                    if z is not None:
                        y = y + z[res:res + RC]
                conv_ref[s * L + r0:s * L + r0 + RC, ch] = y
    y = _rms(conv_ref[...], cn_ref[...])
    c_ref[...] = (y * jax.nn.sigmoid(y)).astype(BF16)


def _pre_call(x2d, cst, an, win, qn, wuq, kvn, inv, ga, gb, cw, cb, cn, *, nb, nt, S, L, pos0, q_transposed):
    rows, d = x2d.shape
    tm = S * L
    conv_ch = cw.shape[1]
    row_blk = lambda w: pl.BlockSpec((tm, w), lambda b, t: (b * nt + t, 0))
    if q_transposed:
        q_shape, q_blk = (N_HEADS * LANES, rows), pl.BlockSpec((N_HEADS * LANES, tm), lambda b, t: (0, b * nt + t))
    else:
        q_shape, q_blk = (rows, N_HEADS * LANES), row_blk(N_HEADS * LANES)
    out_shape = (
        jax.ShapeDtypeStruct((rows, KV_LORA), F32),
        jax.ShapeDtypeStruct((rows, ROPE_DIM), F32),
        jax.ShapeDtypeStruct((rows, 2 * LANES), F32),
        jax.ShapeDtypeStruct(q_shape, BF16),
        jax.ShapeDtypeStruct((rows, conv_ch), BF16),
        jax.ShapeDtypeStruct((nb * S, CONV_HDR, conv_ch), F32),
    )
    state_blk = pl.BlockSpec((S, CONV_HDR, conv_ch), lambda b, t: (b, 0, 0))
    return pl.pallas_call(
        functools.partial(_pre_kernel, S=S, L=L, pos0=pos0, RC=min(L, 64), CB=min(conv_ch, 2 * LANES),
                          q_transposed=q_transposed),
        grid=(nb, nt),
        in_specs=[row_blk(d), state_blk] + [_const_spec(a.shape) for a in (an, win, qn, wuq, kvn, inv, ga, gb, cw, cb, cn)],
        out_specs=(row_blk(KV_LORA), row_blk(ROPE_DIM), row_blk(2 * LANES), q_blk,
                   row_blk(conv_ch), state_blk),
        out_shape=out_shape,
        scratch_shapes=[pltpu.VMEM((S, CONV_HDR + L, conv_ch), F32), pltpu.VMEM((tm, conv_ch), F32)],
        compiler_params=_params(("arbitrary", "arbitrary")),
        name="pre",
    )(x2d, cst, an, win, qn, wuq, kvn, inv, ga, gb, cw, cb, cn)


def _kvup_kernel(ckv_ref, kpe2_ref, wk_ref, wvt_ref, one_ref, inv_ref, ga_ref, gb_ref, k_ref, vt_ref, *, L):
    t = pl.program_id(1)
    ckv = ckv_ref[...].astype(BF16)
    vt = lax.dot_general(wvt_ref[...], ckv, (((1,), (1,)), ((), ())), preferred_element_type=F32)
    vt_ref[...] = (vt + one_ref[...]).astype(BF16)
    kn = _dot(ckv, wk_ref[...])
    pe = kpe2_ref[:, :LANES]
    pe_swapped = kpe2_ref[:, LANES:]
    pos = (t * L + lax.broadcasted_iota(jnp.int32, (L, 1), 0)).astype(F32)
    ta, tb = _rope_tables(pos, inv_ref, ga_ref, gb_ref)
    rot = pe_swapped * tb
    for hd in range(N_HEADS):
        cols = slice(hd * LANES, (hd + 1) * LANES)
        kh = kn[:, cols] + pe
        r = lax.rsqrt(jnp.sum(kh * kh, axis=-1, keepdims=True) * (1.0 / HEAD_DIM) + RMS_EPS)
        k_ref[:, cols] = ((kh * ta + rot) * r).astype(BF16)


def _kvup_call(ckv2d, kpe2, wk, wvt, one, inv, ga, gb, *, nb, nt, L):
    rows = ckv2d.shape[0]
    row_blk = lambda w: pl.BlockSpec((L, w), lambda b, t: (b * nt + t, 0))
    return pl.pallas_call(
        functools.partial(_kvup_kernel, L=L),
        grid=(nb, nt),
        in_specs=[row_blk(KV_LORA), row_blk(2 * LANES)] + [_const_spec(a.shape) for a in (wk, wvt, one, inv, ga, gb)],
        out_specs=(row_blk(N_HEADS * LANES),
                   pl.BlockSpec((None, None, N_HEADS * V_ROWS, L), lambda b, t: (b, t, 0, 0))),
        out_shape=(jax.ShapeDtypeStruct((rows, N_HEADS * LANES), BF16),
                   jax.ShapeDtypeStruct((nb, nt, N_HEADS * V_ROWS, L), BF16)),
        compiler_params=_params(("arbitrary", "arbitrary")),
        name="kvup",
    )(ckv2d, kpe2, wk, wvt, one, inv, ga, gb)


def _finish_heads(accs):
    outs = [a[:V_DIM] / a[V_DIM:V_DIM + 1] for a in accs]
    return jnp.concatenate(outs, axis=0).T


def _attn_prompt_kernel(qt_ref, k_ref, vt_ref, o_ref, m_ref, acc_ref, cmax_ref, s0_ref, s1_ref, *, bq, bk):
    qi = pl.program_id(2)
    half = bk // 2
    diag_blocks = bq // bk
    n_full = qi * diag_blocks
    m_ref[...] = jnp.full(m_ref.shape, NEG_INF, F32)
    acc_ref[...] = jnp.zeros(acc_ref.shape, F32)

    def allowed(k_off):
        q_chunk = lax.broadcasted_iota(jnp.int32, (half, bq), 1) // CHUNK
        k_chunk = (k_off + lax.broadcasted_iota(jnp.int32, (half, bq), 0)) // CHUNK
        return k_chunk <= q_chunk

    def scores(j, part, s_ref):
        start = pl.multiple_of(j * bk + part * half, half)
        for hh in range(2):
            cols = slice(hh * LANES, (hh + 1) * LANES)
            st = _dot(k_ref[pl.ds(start, half), cols], qt_ref[cols, :])
            s_ref[hh] = st
            cmax_ref[part, hh] = jnp.max(st, axis=0, keepdims=True)

    def update(j, part, s_ref, mask):
        for hh in range(2):
            rows = slice(hh * V_ROWS, (hh + 1) * V_ROWS)
            st = s_ref[hh]
            if mask is None:
                cmax = cmax_ref[part, hh]
            else:
                st = jnp.where(mask, st, NEG_INF)
                cmax = jnp.max(st, axis=0, keepdims=True)
            m_old = m_ref[hh]
            m_new = jnp.maximum(m_old, cmax)
            p = jnp.exp2(st - m_new).astype(BF16)
            vt = vt_ref[j, rows, part * half:(part + 1) * half]
            acc_ref[rows, :] = jnp.exp2(m_old - m_new) * acc_ref[rows, :] + _dot(vt, p)
            m_ref[hh] = m_new

    def body(j, carry):
        scores(j, 1, s1_ref)
        update(j, 0, s0_ref, None)
        scores(j + 1, 0, s0_ref)
        update(j, 1, s1_ref, None)
        return carry

    scores(0, 0, s0_ref)
    lax.fori_loop(0, n_full, body, 0)
    for d in range(diag_blocks):
        j = n_full + d
        scores(j, 1, s1_ref)
        update(j, 0, s0_ref, allowed(d * bk))
        if d + 1 < diag_blocks:
            scores(j + 1, 0, s0_ref)
        update(j, 1, s1_ref, allowed(d * bk + half))
    o_ref[...] = _finish_heads([acc_ref[hh * V_ROWS:(hh + 1) * V_ROWS, :] for hh in range(2)]).astype(BF16)


def _attn_prompt_call(qt, k2d, vt, *, B, T, bq, bk):
    nq, nk = T // bq, T // bk
    return pl.pallas_call(
        functools.partial(_attn_prompt_kernel, bq=bq, bk=bk),
        grid=(B, HEAD_PAIRS, nq),
        in_specs=[pl.BlockSpec((2 * LANES, bq), lambda b, hp, qi: (hp, b * nq + qi)),
                  pl.BlockSpec((T, 2 * LANES), lambda b, hp, qi: (b, hp)),
                  pl.BlockSpec((None, nk, 2 * V_ROWS, bk), lambda b, hp, qi: (b, 0, hp, 0))],
        out_specs=pl.BlockSpec((bq, 2 * V_DIM), lambda b, hp, qi: (b * nq + qi, hp)),
        out_shape=jax.ShapeDtypeStruct((B * T, N_HEADS * V_DIM), BF16),
        scratch_shapes=[pltpu.VMEM((2, 1, bq), F32), pltpu.VMEM((2 * V_ROWS, bq), F32),
                        pltpu.VMEM((2, 2, 1, bq), F32),
                        pltpu.VMEM((2, bk // 2, bq), F32), pltpu.VMEM((2, bk // 2, bq), F32)],
        compiler_params=_params(("arbitrary", "arbitrary", "arbitrary")),
        name="attn_prompt",
    )(qt, k2d, vt)


def _attn_block_kernel(q_ref, k_ref, vt_ref, o_ref, *, T, nk, Lk, Tk, pos0):
    Tq = -(-T // LANES) * LANES
    q_chunk = (pos0 + lax.broadcasted_iota(jnp.int32, (Lk, Tq), 1)) // CHUNK
    k_row = lax.broadcasted_iota(jnp.int32, (Lk, Tq), 0)
    accs = []
    for hh in range(2):
        cols = slice(hh * LANES, (hh + 1) * LANES)
        rows = slice(hh * V_ROWS, (hh + 1) * V_ROWS)
        q = q_ref[:, cols]
        if Tq > T:
            q = jnp.concatenate([q, jnp.zeros((Tq - T, LANES), BF16)], axis=0)
        sts = []
        for j in range(nk):
            st = lax.dot_general(k_ref[j * Lk:(j + 1) * Lk, cols], q, (((1,), (1,)), ((), ())),
                                 preferred_element_type=F32)
            k_pos = j * Lk + k_row
            sts.append(jnp.where((k_pos // CHUNK <= q_chunk) & (k_pos < Tk), st, NEG_INF))
        m = functools.reduce(jnp.maximum, [jnp.max(st, axis=0, keepdims=True) for st in sts])
        accs.append(sum(_dot(vt_ref[j, rows, :], jnp.exp2(sts[j] - m).astype(BF16)) for j in range(nk)))
    o_ref[...] = _finish_heads(accs)[:T].astype(BF16)


def _attn_block_call(q2d, k2d, vt, *, B, T, nk, Lk, Tk, pos0):
    return pl.pallas_call(
        functools.partial(_attn_block_kernel, T=T, nk=nk, Lk=Lk, Tk=Tk, pos0=pos0),
        grid=(B, HEAD_PAIRS),
        in_specs=[pl.BlockSpec((T, 2 * LANES), lambda b, hp: (b, hp)),
                  pl.BlockSpec((nk * Lk, 2 * LANES), lambda b, hp: (b, hp)),
                  pl.BlockSpec((None, nk, 2 * V_ROWS, Lk), lambda b, hp: (b, 0, hp, 0))],
        out_specs=pl.BlockSpec((T, 2 * V_DIM), lambda b, hp: (b, hp)),
        out_shape=jax.ShapeDtypeStruct((B * T, N_HEADS * V_DIM), BF16),
        compiler_params=_params(("arbitrary", "arbitrary")),
        name="attn_block",
    )(q2d, k2d, vt)


def _post_kernel(x_ref, att_ref, c_ref, fst_ref, wout_ref, fn_ref, wup_ref, fw_ref, fb_ref, wdn_ref,
                 y_ref, fstate_ref, ext_ref, *, S, L, FC):
    t = pl.program_id(1)
    att_w = att_ref.shape[1]
    d_ff = wdn_ref.shape[0]

    @pl.when(t == 0)
    def _():
        fstate_ref[...] = fst_ref[...]

    x2 = x_ref[...] + _dot(att_ref[...], wout_ref[:att_w, :]) + _dot(c_ref[...], wout_ref[att_w:, :])
    h2 = _rms(x2, fn_ref[...]).astype(BF16)
    y_ref[...] = x2
    for j in range(d_ff // FC):
        cols = slice(j * FC, (j + 1) * FC)
        a = _dot(h2, wup_ref[:, cols])
        gate = _dot(h2, wup_ref[:, d_ff + j * FC:d_ff + (j + 1) * FC])
        for s in range(S):
            ext_ref[s, :FFN_HDR, :] = fstate_ref[s, :, cols]
            ext_ref[s, FFN_HDR:, :] = a[s * L:(s + 1) * L]
            fstate_ref[s, :, cols] = a[(s + 1) * L - FFN_HDR:(s + 1) * L]
        acts = []
        for s in range(S):
            conv = fb_ref[:, cols] + fw_ref[2:3, cols] * a[s * L:(s + 1) * L]
            for k in range(FFN_K - 1):
                off = FFN_HDR - (FFN_K - 1) + k
                conv = conv + fw_ref[k:k + 1, cols] * ext_ref[s, off:off + L, :]
            acts.append((conv * jax.nn.sigmoid(conv) * gate[s * L:(s + 1) * L]).astype(BF16))
        act = acts[0] if S == 1 else jnp.concatenate(acts, axis=0)
        y_ref[...] += _dot(act, wdn_ref[cols, :])


def _post_call(x2d, att, c, fst, wout, fn, wup, fw, fb, wdn, *, nb, nt, S, L, FC):
    rows, d = x2d.shape
    tm = S * L
    d_ff = wdn.shape[0]
    row_blk = lambda w: pl.BlockSpec((tm, w), lambda b, t: (b * nt + t, 0))
    state_blk = pl.BlockSpec((S, FFN_HDR, d_ff), lambda b, t: (b, 0, 0))
    return pl.pallas_call(
        functools.partial(_post_kernel, S=S, L=L, FC=FC),
        grid=(nb, nt),
        in_specs=[row_blk(d), row_blk(att.shape[1]), row_blk(c.shape[1]), state_blk]
        + [_const_spec(a.shape) for a in (wout, fn, wup, fw, fb, wdn)],
        out_specs=(row_blk(d), state_blk),
        out_shape=(jax.ShapeDtypeStruct((rows, d), F32), jax.ShapeDtypeStruct((nb * S, FFN_HDR, d_ff), F32)),
        scratch_shapes=[pltpu.VMEM((S, FFN_HDR + L, FC), F32)],
        compiler_params=_params(("arbitrary", "arbitrary")),
        name="post",
    )(x2d, att, c, fst, wout, fn, wup, fw, fb, wdn)


def _head_groups(w, n_rows):
    w = w.reshape(n_rows, N_HEADS, HEAD_DIM)
    nope, pe = w[..., :QK_NOPE], w[..., QK_NOPE:]
    pe_sw = jnp.concatenate([pe[..., ROPE_DIM // 2:], pe[..., :ROPE_DIM // 2]], axis=-1)
    pad = jnp.zeros((n_rows, N_HEADS, LANES - HEAD_DIM), w.dtype)
    straight = jnp.concatenate([nope, pe, pad], axis=-1).reshape(n_rows, N_HEADS * LANES)
    swapped = jnp.concatenate([jnp.zeros_like(nope), pe_sw, pad], axis=-1).reshape(n_rows, N_HEADS * LANES)
    return straight, swapped


def _rope_lane_consts(g):
    half = ROPE_DIM // 2
    inv = 1.0 / (ROPE_BASE ** (jnp.arange(0, ROPE_DIM, 2, dtype=F32) / ROPE_DIM))
    z = lambda n: jnp.zeros((n,), F32)
    inv_lane = jnp.concatenate([z(QK_NOPE), inv, inv, z(LANES - HEAD_DIM)])
    ga = jnp.concatenate([g, z(LANES - HEAD_DIM)])
    gb = jnp.concatenate([z(QK_NOPE), -g[QK_NOPE + half:], g[QK_NOPE:QK_NOPE + half], z(LANES - HEAD_DIM)])
    return inv_lane[None], ga[None], gb[None]


def _place_pe(pe):
    half = ROPE_DIM // 2
    z = lambda n: jnp.zeros(pe.shape[:-1] + (n,), pe.dtype)
    pe_sw = jnp.concatenate([pe[..., half:], pe[..., :half]], axis=-1)
    return jnp.concatenate([z(QK_NOPE), pe, z(LANES - HEAD_DIM), z(QK_NOPE), pe_sw, z(LANES - HEAD_DIM)], axis=-1)


def _prep_weights(w_in, w_uq, w_ukv, w_out, w_up, w_down):
    d = w_in.shape[0]
    o = Q_LORA + KV_LORA
    w_pe = w_in[:, o:o + ROPE_DIM]
    pe_out = jnp.concatenate([w_pe, jnp.zeros((d, LANES - ROPE_DIM), w_in.dtype)], axis=-1)
    win = jnp.concatenate([w_in[:, :o], pe_out, _place_pe(w_pe), w_in[:, o + ROPE_DIM:]], axis=-1).astype(BF16)
    wuq = jnp.concatenate(_head_groups(w_uq, Q_LORA), axis=-1).astype(BF16)
    wkv = w_ukv.reshape(KV_LORA, N_HEADS, QK_NOPE + V_DIM)
    wk = jnp.concatenate([wkv[..., :QK_NOPE], jnp.zeros((KV_LORA, N_HEADS, LANES - QK_NOPE), w_ukv.dtype)], axis=-1)
    wk = wk.reshape(KV_LORA, N_HEADS * LANES).astype(BF16)
    wvt = jnp.transpose(wkv[..., QK_NOPE:], (1, 2, 0))
    wvt = jnp.pad(wvt, ((0, 0), (0, V_ROWS - V_DIM), (0, 0))).reshape(N_HEADS * V_ROWS, KV_LORA).astype(BF16)
    one = jnp.zeros((N_HEADS, V_ROWS, 1), F32).at[:, V_DIM].set(1.0).reshape(N_HEADS * V_ROWS, 1)
    return win, wuq, wk, wvt, one, w_out.astype(BF16), w_up.astype(BF16), w_down.astype(BF16)


def _pad_front(a, n):
    return jnp.pad(a, ((0, 0), (n - a.shape[1], 0), (0, 0)))


def _layer(x, ckv_past, kpe_past, conv_past, ffn_past, w, *, tile_rows, kv_rows, attn_bq, attn_bk, seg_per_tile,
           ffn_chunk):
    (attn_norm, w_in, q_norm, w_uq, kv_norm, w_ukv, qk_norm_q, qk_norm_k, conv_w, conv_b, conv_norm,
     w_out, ffn_norm, w_up, ffn_conv_w, ffn_conv_b, w_down) = w
    B, T, D = x.shape
    pos0 = ckv_past.shape[1]
    Tk = pos0 + T
    win, wuq, wk, wvt, one, wout, wup, wdn = _prep_weights(w_in, w_uq, w_ukv, w_out, w_up, w_down)
    inv_q, ga_q, gb_q = _rope_lane_consts(qk_norm_q)
    inv_k, ga_k, gb_k = _rope_lane_consts(qk_norm_k)
    row = lambda v: v[None].astype(F32)

    if T >= tile_rows:
        S, L, nb, nt = 1, tile_rows, B, T // tile_rows
    else:
        S, L, nb, nt = seg_per_tile, T, B // seg_per_tile, 1
    blocked = pos0 == 0 and T % attn_bq == 0
    if blocked:
        Lk, nk = attn_bk, T // attn_bk
    else:
        nk = -(-Tk // kv_rows)
        Lk = -(-Tk // (nk * LANES)) * LANES
    pad_k = nk * Lk - Tk

    x2d = x.reshape(B * T, D)
    ckv, kpe, kpe2, q, c, cstate = _pre_call(
        x2d, _pad_front(conv_past, CONV_HDR), row(attn_norm), win, row(q_norm), wuq, row(kv_norm),
        inv_q, ga_q, gb_q, conv_w, row(conv_b), row(conv_norm), nb=nb, nt=nt, S=S, L=L, pos0=pos0,
        q_transposed=blocked)

    seq = lambda past, new, w: jnp.concatenate(
        [past, new.reshape(B, T, w), jnp.zeros((B, pad_k, w), new.dtype)], axis=1).reshape(B * nk * Lk, w)
    k, vt = _kvup_call(seq(ckv_past, ckv, KV_LORA), seq(_place_pe(kpe_past), kpe2, 2 * LANES), wk, wvt, one,
                       inv_k, ga_k, gb_k, nb=B, nt=nk, L=Lk)

    if blocked:
        att = _attn_prompt_call(q, k, vt, B=B, T=T, bq=attn_bq, bk=attn_bk)
    else:
        att = _attn_block_call(q, k, vt, B=B, T=T, nk=nk, Lk=Lk, Tk=Tk, pos0=pos0)

    y, fstate = _post_call(x2d, att, c, _pad_front(ffn_past, FFN_HDR), wout, row(ffn_norm), wup,
                           ffn_conv_w, row(ffn_conv_b), wdn, nb=nb, nt=nt, S=S, L=L, FC=ffn_chunk)
    return (y.reshape(B, T, D), ckv.reshape(B, T, KV_LORA), kpe.reshape(B, T, ROPE_DIM),
            cstate[:, CONV_HDR - (CONV_K - 1):], fstate[:, FFN_HDR - (FFN_K - 1):])


def kernel(x_prompt, x_sample, cache_ckv, cache_kpe, state_conv, state_ffn_conv, attn_norm, w_in, q_norm, w_uq,
           kv_norm, w_ukv, qk_norm_q, qk_norm_k, conv_w, conv_b, conv_norm, w_out, ffn_norm, w_up, ffn_conv_w,
           ffn_conv_b, w_down):
    depth = w_in.shape[0]
    B = x_prompt.shape[0]
    dt = x_prompt.dtype
    d_ff = w_down.shape[1]
    conv_ch = conv_w.shape[2]
    cfg = dict(tile_rows=512, kv_rows=1024, attn_bq=512, attn_bk=512, seg_per_tile=4, ffn_chunk=256)
    yp, ys = x_prompt, x_sample
    outs_p, outs_s = [], []
    for l in range(depth):
        w = (attn_norm[l], w_in[l], q_norm[l], w_uq[l], kv_norm[l], w_ukv[l], qk_norm_q[l], qk_norm_k[l],
             conv_w[l], conv_b[l], conv_norm[l], w_out[l], ffn_norm[l], w_up[l], ffn_conv_w[l], ffn_conv_b[l],
             w_down[l])
        yp, *rest_p = _layer(yp, jnp.zeros((B, 0, KV_LORA), dt), jnp.zeros((B, 0, ROPE_DIM), dt),
                             jnp.zeros((B, CONV_K - 1, conv_ch), dt), jnp.zeros((B, FFN_K - 1, d_ff), dt), w, **cfg)
        ys, *rest_s = _layer(ys, cache_ckv[l], cache_kpe[l], state_conv[l], state_ffn_conv[l], w, **cfg)
        outs_p.append(rest_p)
        outs_s.append(rest_s)
    stack = lambda outs, i: jnp.stack([o[i] for o in outs])
    return (yp, ys, stack(outs_p, 0), stack(outs_p, 1), stack(outs_p, 2), stack(outs_p, 3),
            stack(outs_s, 0), stack(outs_s, 1), stack(outs_s, 2), stack(outs_s, 3))
```

```python
import functools

import jax
import jax.numpy as jnp
from jax import lax
from jax.experimental import pallas as pl
from jax.experimental.pallas import tpu as pltpu

CHUNK = 64
N_HEADS = 8
QK_NOPE = 64
ROPE_DIM = 32
HEAD_DIM = QK_NOPE + ROPE_DIM
V_DIM = 64
Q_LORA = 384
KV_LORA = 256
CONV_K = 31
FFN_K = 3
ROPE_BASE = 10000.0
RMS_EPS = 1e-6
NEG_INF = -1e30
SCALE = HEAD_DIM ** -0.5
Q_SCALE = SCALE * 1.4426950408889634

LANES = 128
SUBLANES = 8
HEAD_PAIRS = N_HEADS // 2
V_ROWS = 80
CONV_HDR = 32
FFN_HDR = 8
VMEM_LIMIT = 56 * 1024 * 1024

F32 = jnp.float32
BF16 = jnp.bfloat16


def _rms(x, g):
    return x * lax.rsqrt(jnp.mean(x * x, axis=-1, keepdims=True) + RMS_EPS) * g


def _dot(a, b):
    return jnp.dot(a, b, preferred_element_type=F32)


def _init_rope_scratch(L, cos_ref, sin_ref, inv_ref):
    @pl.when((pl.program_id(0) == 0) & (pl.program_id(1) == 0))
    def _():
        ang = lax.broadcasted_iota(jnp.int32, (L, 1), 0).astype(F32) * inv_ref[...]
        cos_ref[...] = jnp.cos(ang)
        sin_ref[...] = jnp.sin(ang)


def _rope_tables(base_pos, cos_ref, sin_ref, inv_ref, ga_ref, gb_ref):
    base = base_pos.astype(F32) * inv_ref[...]
    c0, s0 = jnp.cos(base), jnp.sin(base)
    ci, si = cos_ref[...], sin_ref[...]
    return ga_ref[...] * (c0 * ci - s0 * si), gb_ref[...] * (s0 * ci + c0 * si)


def _params(sem):
    return pltpu.CompilerParams(dimension_semantics=sem, vmem_limit_bytes=VMEM_LIMIT)


def _const_spec(shape):
    nd = len(shape)
    return pl.BlockSpec(shape, lambda *_: (0,) * nd)


def _pre_kernel(x_ref, cst_ref, an_ref, win_ref, qn_ref, wuq_ref, kvn_ref, inv_ref, ga_ref, gb_ref,
                cw_ref, cb_ref, cn_ref,
                ckv_ref, kpe_ref, q_ref, c_ref, cstate_ref, ext_ref, conv_ref, cos_ref, sin_ref,
                *, S, L, pos0, RC, CB,
                q_transposed):
    t = pl.program_id(1)
    conv_ch = cw_ref.shape[1]
    _init_rope_scratch(L, cos_ref, sin_ref, inv_ref)

    x = x_ref[...]
    h = _rms(x, an_ref[...]).astype(BF16)

    c_q = _dot(h, win_ref[:, :Q_LORA])
    hq = _rms(c_q, qn_ref[...]).astype(BF16)
    c_kv = _dot(h, win_ref[:, Q_LORA:Q_LORA + KV_LORA])
    ckv_ref[...] = _rms(c_kv, kvn_ref[...])
    o = Q_LORA + KV_LORA
    kpe_ref[...] = _dot(h, win_ref[:, o:o + LANES])[:, :ROPE_DIM]
    o += LANES
    glu_a = _dot(h, win_ref[:, o:o + conv_ch])
    glu_b = _dot(h, win_ref[:, o + conv_ch:o + 2 * conv_ch])
    u = glu_a * jax.nn.sigmoid(glu_b)

    qa = _dot(hq, wuq_ref[:, :N_HEADS * LANES])
    qb = _dot(hq, wuq_ref[:, N_HEADS * LANES:])
    ta, tb = _rope_tables(pos0 + t * L, cos_ref, sin_ref, inv_ref, ga_ref, gb_ref)
    for s in range(S):
        rows = slice(s * L, (s + 1) * L)
        for hd in range(N_HEADS):
            cols = slice(hd * LANES, (hd + 1) * LANES)
            qh = qa[rows, cols]
            r = lax.rsqrt(jnp.sum(qh * qh, axis=-1, keepdims=True) * (1.0 / HEAD_DIM) + RMS_EPS) * Q_SCALE
            qr = (qh * ta + qb[rows, cols] * tb) * r
            if q_transposed:
                q_ref[cols, rows] = qr.T.astype(BF16)
            else:
                q_ref[rows, cols] = qr.astype(BF16)

    @pl.when(t == 0)
    def _():
        ext_ref[:, :CONV_HDR, :] = cst_ref[...]

    @pl.when(t > 0)
    def _():
        ext_ref[:, :CONV_HDR, :] = ext_ref[:, L:L + CONV_HDR, :]

    for s in range(S):
        ext_ref[s, CONV_HDR:, :] = u[s * L:(s + 1) * L]
    cstate_ref[...] = ext_ref[:, L:L + CONV_HDR, :]

    first = CONV_HDR - (CONV_K - 1)
    for s in range(S):
        for r0 in range(0, L, RC):
            for c0 in range(0, conv_ch, CB):
                ch = slice(c0, c0 + CB)
                y = jnp.broadcast_to(cb_ref[:, ch], (RC, CB))
                for res in range(SUBLANES):
                    taps = [k for k in range(CONV_K) if (first + k) % SUBLANES == res]
                    n_rows = RC + (SUBLANES if res else 0)
                    z = None
                    for k in taps:
                        a0 = r0 + first + k - res
                        term = cw_ref[k:k + 1, ch] * ext_ref[s, a0:a0 + n_rows, ch]
                        z = term if z is None else z + term
                    if z is not None:
                        y = y + z[res:res + RC]
                conv_ref[s * L + r0:s * L + r0 + RC, ch] = y
    y = _rms(conv_ref[...], cn_ref[...])
    c_ref[...] = (y * jax.nn.sigmoid(y)).astype(BF16)


def _pre_call(x2d, cst, an, win, qn, wuq, kvn, inv, ga, gb, cw, cb, cn, *, nb, nt, S, L, pos0, q_transposed):
    rows, d = x2d.shape
    tm = S * L
    conv_ch = cw.shape[1]
    row_blk = lambda w: pl.BlockSpec((tm, w), lambda b, t: (b * nt + t, 0))
    if q_transposed:
        q_shape, q_blk = (N_HEADS * LANES, rows), pl.BlockSpec((N_HEADS * LANES, tm), lambda b, t: (0, b * nt + t))
    else:
        q_shape, q_blk = (rows, N_HEADS * LANES), row_blk(N_HEADS * LANES)
    out_shape = (
        jax.ShapeDtypeStruct((rows, KV_LORA), F32),
        jax.ShapeDtypeStruct((rows, ROPE_DIM), F32),
        jax.ShapeDtypeStruct(q_shape, BF16),
        jax.ShapeDtypeStruct((rows, conv_ch), BF16),
        jax.ShapeDtypeStruct((nb * S, CONV_HDR, conv_ch), F32),
    )
    state_blk = pl.BlockSpec((S, CONV_HDR, conv_ch), lambda b, t: (b, 0, 0))
    return pl.pallas_call(
        functools.partial(_pre_kernel, S=S, L=L, pos0=pos0, RC=min(L, 64), CB=min(conv_ch, 2 * LANES),
                          q_transposed=q_transposed),
        grid=(nb, nt),
        in_specs=[row_blk(d), state_blk] + [_const_spec(a.shape) for a in (an, win, qn, wuq, kvn, inv, ga, gb, cw, cb, cn)],
        out_specs=(row_blk(KV_LORA), row_blk(ROPE_DIM), q_blk, row_blk(conv_ch), state_blk),
        out_shape=out_shape,
        scratch_shapes=[pltpu.VMEM((S, CONV_HDR + L, conv_ch), F32), pltpu.VMEM((tm, conv_ch), F32),
                        pltpu.VMEM((L, LANES), F32), pltpu.VMEM((L, LANES), F32)],
        compiler_params=_params(("arbitrary", "arbitrary")),
        name="pre",
    )(x2d, cst, an, win, qn, wuq, kvn, inv, ga, gb, cw, cb, cn)


def _kvup_kernel(ckv_ref, kpe_ref, wk_ref, wvt_ref, one_ref, inv_ref, ga_ref, gb_ref, k_ref, vt_ref,
                 cos_ref, sin_ref, pe_ref, *, L):
    t = pl.program_id(1)
    _init_rope_scratch(L, cos_ref, sin_ref, inv_ref)

    @pl.when((pl.program_id(0) == 0) & (t == 0))
    def _():
        pe_ref[...] = jnp.zeros(pe_ref.shape, F32)

    ckv = ckv_ref[...].astype(BF16)
    vt = lax.dot_general(wvt_ref[...], ckv, (((1,), (1,)), ((), ())), preferred_element_type=F32)
    vt_ref[...] = (vt + one_ref[...]).astype(BF16)
    kn = _dot(ckv, wk_ref[...])
    pe_ref[:, :ROPE_DIM] = kpe_ref[...]
    x = pe_ref[...]
    half = ROPE_DIM // 2
    lane = lax.broadcasted_iota(jnp.int32, (L, LANES), 1)
    pe = pltpu.roll(x, QK_NOPE, axis=1)
    pe_swapped = jnp.where(lane < QK_NOPE + half, pltpu.roll(x, QK_NOPE - half, axis=1),
                           pltpu.roll(x, QK_NOPE + half, axis=1))
    ta, tb = _rope_tables(t * L, cos_ref, sin_ref, inv_ref, ga_ref, gb_ref)
    rot = pe_swapped * tb
    for hd in range(N_HEADS):
        cols = slice(hd * LANES, (hd + 1) * LANES)
        kh = kn[:, cols] + pe
        r = lax.rsqrt(jnp.sum(kh * kh, axis=-1, keepdims=True) * (1.0 / HEAD_DIM) + RMS_EPS)
        k_ref[:, cols] = ((kh * ta + rot) * r).astype(BF16)


def _kvup_call(ckv2d, kpe2d, wk, wvt, one, inv, ga, gb, *, nb, nt, L):
    rows = ckv2d.shape[0]
    row_blk = lambda w: pl.BlockSpec((L, w), lambda b, t: (b * nt + t, 0))
    return pl.pallas_call(
        functools.partial(_kvup_kernel, L=L),
        grid=(nb, nt),
        in_specs=[row_blk(KV_LORA), row_blk(ROPE_DIM)] + [_const_spec(a.shape) for a in (wk, wvt, one, inv, ga, gb)],
        out_specs=(row_blk(N_HEADS * LANES),
                   pl.BlockSpec((None, None, N_HEADS * V_ROWS, L), lambda b, t: (b, t, 0, 0))),
        out_shape=(jax.ShapeDtypeStruct((rows, N_HEADS * LANES), BF16),
                   jax.ShapeDtypeStruct((nb, nt, N_HEADS * V_ROWS, L), BF16)),
        scratch_shapes=[pltpu.VMEM((L, LANES), F32), pltpu.VMEM((L, LANES), F32), pltpu.VMEM((L, LANES), F32)],
        compiler_params=_params(("arbitrary", "arbitrary")),
        name="kvup",
    )(ckv2d, kpe2d, wk, wvt, one, inv, ga, gb)


def _finish_heads(accs):
    outs = [a[:V_DIM] / a[V_DIM:V_DIM + 1] for a in accs]
    return jnp.concatenate(outs, axis=0).T


def _attn_prompt_kernel(qt_ref, k_ref, vt_ref, o_ref, m_ref, acc_ref, cmax_ref, s0_ref, s1_ref, *, bq, bk):
    qi = pl.program_id(2)
    half = bk // 2
    diag_blocks = bq // bk
    n_full = qi * diag_blocks
    m_ref[...] = jnp.full(m_ref.shape, NEG_INF, F32)
    acc_ref[...] = jnp.zeros(acc_ref.shape, F32)

    def allowed(k_off):
        q_chunk = lax.broadcasted_iota(jnp.int32, (half, bq), 1) // CHUNK
        k_chunk = (k_off + lax.broadcasted_iota(jnp.int32, (half, bq), 0)) // CHUNK
        return k_chunk <= q_chunk

    def scores(j, part, s_ref):
        start = pl.multiple_of(j * bk + part * half, half)
        for hh in range(2):
            cols = slice(hh * LANES, (hh + 1) * LANES)
            st = _dot(k_ref[pl.ds(start, half), cols], qt_ref[cols, :])
            s_ref[hh] = st
            cmax_ref[part, hh] = jnp.max(st, axis=0, keepdims=True)

    def update(j, part, s_ref, mask):
        for hh in range(2):
            rows = slice(hh * V_ROWS, (hh + 1) * V_ROWS)
            st = s_ref[hh]
            if mask is None:
                cmax = cmax_ref[part, hh]
            else:
                st = jnp.where(mask, st, NEG_INF)
                cmax = jnp.max(st, axis=0, keepdims=True)
            m_old = m_ref[hh]
            m_new = jnp.maximum(m_old, cmax)
            p = jnp.exp2(st - m_new).astype(BF16)
            vt = vt_ref[j, rows, part * half:(part + 1) * half]
            acc_ref[rows, :] = jnp.exp2(m_old - m_new) * acc_ref[rows, :] + _dot(vt, p)
            m_ref[hh] = m_new

    def body(j, carry):
        scores(j, 1, s1_ref)
        update(j, 0, s0_ref, None)
        scores(j + 1, 0, s0_ref)
        update(j, 1, s1_ref, None)
        return carry

    scores(0, 0, s0_ref)
    lax.fori_loop(0, n_full, body, 0)
    for d in range(diag_blocks):
        j = n_full + d
        scores(j, 1, s1_ref)
        update(j, 0, s0_ref, allowed(d * bk))
        if d + 1 < diag_blocks:
            scores(j + 1, 0, s0_ref)
        update(j, 1, s1_ref, allowed(d * bk + half))
    o_ref[...] = _finish_heads([acc_ref[hh * V_ROWS:(hh + 1) * V_ROWS, :] for hh in range(2)]).astype(BF16)


def _attn_prompt_call(qt, k2d, vt, *, B, T, bq, bk):
    nq, nk = T // bq, T // bk
    return pl.pallas_call(
        functools.partial(_attn_prompt_kernel, bq=bq, bk=bk),
        grid=(B, HEAD_PAIRS, nq),
        in_specs=[pl.BlockSpec((2 * LANES, bq), lambda b, hp, qi: (hp, b * nq + qi)),
                  pl.BlockSpec((T, 2 * LANES), lambda b, hp, qi: (b, hp)),
                  pl.BlockSpec((None, nk, 2 * V_ROWS, bk), lambda b, hp, qi: (b, 0, hp, 0))],
        out_specs=pl.BlockSpec((bq, 2 * V_DIM), lambda b, hp, qi: (b * nq + qi, hp)),
        out_shape=jax.ShapeDtypeStruct((B * T, N_HEADS * V_DIM), BF16),
        scratch_shapes=[pltpu.VMEM((2, 1, bq), F32), pltpu.VMEM((2 * V_ROWS, bq), F32),
                        pltpu.VMEM((2, 2, 1, bq), F32),
                        pltpu.VMEM((2, bk // 2, bq), F32), pltpu.VMEM((2, bk // 2, bq), F32)],
        compiler_params=_params(("arbitrary", "arbitrary", "arbitrary")),
        name="attn_prompt",
    )(qt, k2d, vt)


def _attn_block_kernel(q_ref, k_ref, vt_ref, o_ref, *, T, nk, Lk, Tk, pos0):
    Tq = -(-T // LANES) * LANES
    q_chunk = (pos0 + lax.broadcasted_iota(jnp.int32, (Lk, Tq), 1)) // CHUNK
    k_row = lax.broadcasted_iota(jnp.int32, (Lk, Tq), 0)
    accs = []
    for hh in range(2):
        cols = slice(hh * LANES, (hh + 1) * LANES)
        rows = slice(hh * V_ROWS, (hh + 1) * V_ROWS)
        q = q_ref[:, cols]
        if Tq > T:
            q = jnp.concatenate([q, jnp.zeros((Tq - T, LANES), BF16)], axis=0)
        sts = []
        for j in range(nk):
            st = lax.dot_general(k_ref[j * Lk:(j + 1) * Lk, cols], q, (((1,), (1,)), ((), ())),
                                 preferred_element_type=F32)
            k_pos = j * Lk + k_row
            sts.append(jnp.where((k_pos // CHUNK <= q_chunk) & (k_pos < Tk), st, NEG_INF))
        m = functools.reduce(jnp.maximum, [jnp.max(st, axis=0, keepdims=True) for st in sts])
        accs.append(sum(_dot(vt_ref[j, rows, :], jnp.exp2(sts[j] - m).astype(BF16)) for j in range(nk)))
    o_ref[...] = _finish_heads(accs)[:T].astype(BF16)


def _attn_block_call(q2d, k2d, vt, *, B, T, nk, Lk, Tk, pos0):
    return pl.pallas_call(
        functools.partial(_attn_block_kernel, T=T, nk=nk, Lk=Lk, Tk=Tk, pos0=pos0),
        grid=(B, HEAD_PAIRS),
        in_specs=[pl.BlockSpec((T, 2 * LANES), lambda b, hp: (b, hp)),
                  pl.BlockSpec((nk * Lk, 2 * LANES), lambda b, hp: (b, hp)),
                  pl.BlockSpec((None, nk, 2 * V_ROWS, Lk), lambda b, hp: (b, 0, hp, 0))],
        out_specs=pl.BlockSpec((T, 2 * V_DIM), lambda b, hp: (b, hp)),
        out_shape=jax.ShapeDtypeStruct((B * T, N_HEADS * V_DIM), BF16),
        compiler_params=_params(("arbitrary", "arbitrary")),
        name="attn_block",
    )(q2d, k2d, vt)


def _post_kernel(x_ref, att_ref, c_ref, fst_ref, wout_ref, fn_ref, wup_ref, fw_ref, fb_ref, wdn_ref,
                 y_ref, fstate_ref, ext_ref, *, S, L, FC):
    t = pl.program_id(1)
    att_w = att_ref.shape[1]
    d_ff = wdn_ref.shape[0]

    @pl.when(t == 0)
    def _():
        fstate_ref[...] = fst_ref[...]

    x2 = x_ref[...] + _dot(att_ref[...], wout_ref[:att_w, :]) + _dot(c_ref[...], wout_ref[att_w:, :])
    h2 = _rms(x2, fn_ref[...]).astype(BF16)
    y_ref[...] = x2
    for c0 in range(0, d_ff, FC):
        w = min(FC, d_ff - c0)
        cols = slice(c0, c0 + w)
        a = _dot(h2, wup_ref[:, cols])
        gate = _dot(h2, wup_ref[:, d_ff + c0:d_ff + c0 + w])
        for s in range(S):
            ext_ref[s, :FFN_HDR, :w] = fstate_ref[s, :, cols]
            ext_ref[s, FFN_HDR:, :w] = a[s * L:(s + 1) * L]
            fstate_ref[s, :, cols] = a[(s + 1) * L - FFN_HDR:(s + 1) * L]
        acts = []
        for s in range(S):
            conv = fb_ref[:, cols] + fw_ref[2:3, cols] * a[s * L:(s + 1) * L]
            for k in range(FFN_K - 1):
                off = FFN_HDR - (FFN_K - 1) + k
                conv = conv + fw_ref[k:k + 1, cols] * ext_ref[s, off:off + L, :w]
            acts.append((conv * jax.nn.sigmoid(conv) * gate[s * L:(s + 1) * L]).astype(BF16))
        act = acts[0] if S == 1 else jnp.concatenate(acts, axis=0)
        y_ref[...] += _dot(act, wdn_ref[cols, :])


def _post_call(x2d, att, c, fst, wout, fn, wup, fw, fb, wdn, *, nb, nt, S, L, FC):
    rows, d = x2d.shape
    tm = S * L
    d_ff = wdn.shape[0]
    row_blk = lambda w: pl.BlockSpec((tm, w), lambda b, t: (b * nt + t, 0))
    state_blk = pl.BlockSpec((S, FFN_HDR, d_ff), lambda b, t: (b, 0, 0))
    return pl.pallas_call(
        functools.partial(_post_kernel, S=S, L=L, FC=FC),
        grid=(nb, nt),
        in_specs=[row_blk(d), row_blk(att.shape[1]), row_blk(c.shape[1]), state_blk]
        + [_const_spec(a.shape) for a in (wout, fn, wup, fw, fb, wdn)],
        out_specs=(row_blk(d), state_blk),
        out_shape=(jax.ShapeDtypeStruct((rows, d), F32), jax.ShapeDtypeStruct((nb * S, FFN_HDR, d_ff), F32)),
        scratch_shapes=[pltpu.VMEM((S, FFN_HDR + L, FC), F32)],
        compiler_params=_params(("arbitrary", "arbitrary")),
        name="post",
    )(x2d, att, c, fst, wout, fn, wup, fw, fb, wdn)


def _head_groups(w, n_rows):
    w = w.reshape(n_rows, N_HEADS, HEAD_DIM)
    nope, pe = w[..., :QK_NOPE], w[..., QK_NOPE:]
    pe_sw = jnp.concatenate([pe[..., ROPE_DIM // 2:], pe[..., :ROPE_DIM // 2]], axis=-1)
    pad = jnp.zeros((n_rows, N_HEADS, LANES - HEAD_DIM), w.dtype)
    straight = jnp.concatenate([nope, pe, pad], axis=-1).reshape(n_rows, N_HEADS * LANES)
    swapped = jnp.concatenate([jnp.zeros_like(nope), pe_sw, pad], axis=-1).reshape(n_rows, N_HEADS * LANES)
    return straight, swapped


def _rope_lane_consts(g):
    half = ROPE_DIM // 2
    inv = 1.0 / (ROPE_BASE ** (jnp.arange(0, ROPE_DIM, 2, dtype=F32) / ROPE_DIM))
    z = lambda n: jnp.zeros((n,), F32)
    inv_lane = jnp.concatenate([z(QK_NOPE), inv, inv, z(LANES - HEAD_DIM)])
    ga = jnp.concatenate([g, z(LANES - HEAD_DIM)])
    gb = jnp.concatenate([z(QK_NOPE), -g[QK_NOPE + half:], g[QK_NOPE:QK_NOPE + half], z(LANES - HEAD_DIM)])
    return inv_lane[None], ga[None], gb[None]


def _prep_weights(w_in, w_uq, w_ukv, w_out, w_up, w_down):
    d = w_in.shape[0]
    o = Q_LORA + KV_LORA
    pe_out = jnp.concatenate([w_in[:, o:o + ROPE_DIM], jnp.zeros((d, LANES - ROPE_DIM), w_in.dtype)], axis=-1)
    win = jnp.concatenate([w_in[:, :o], pe_out, w_in[:, o + ROPE_DIM:]], axis=-1).astype(BF16)
    wuq = jnp.concatenate(_head_groups(w_uq, Q_LORA), axis=-1).astype(BF16)
    wkv = w_ukv.reshape(KV_LORA, N_HEADS, QK_NOPE + V_DIM)
    wk = jnp.concatenate([wkv[..., :QK_NOPE], jnp.zeros((KV_LORA, N_HEADS, LANES - QK_NOPE), w_ukv.dtype)], axis=-1)
    wk = wk.reshape(KV_LORA, N_HEADS * LANES).astype(BF16)
    wvt = jnp.transpose(wkv[..., QK_NOPE:], (1, 2, 0))
    wvt = jnp.pad(wvt, ((0, 0), (0, V_ROWS - V_DIM), (0, 0))).reshape(N_HEADS * V_ROWS, KV_LORA).astype(BF16)
    one = jnp.zeros((N_HEADS, V_ROWS, 1), F32).at[:, V_DIM].set(1.0).reshape(N_HEADS * V_ROWS, 1)
    return win, wuq, wk, wvt, one, w_out.astype(BF16), w_up.astype(BF16), w_down.astype(BF16)


def _pad_front(a, n):
    return jnp.pad(a, ((0, 0), (n - a.shape[1], 0), (0, 0)))


def _layer(x, ckv_past, kpe_past, conv_past, ffn_past, w, *, tile_rows, kv_rows, attn_bq, attn_bk, seg_per_tile,
           ffn_chunk):
    (attn_norm, w_in, q_norm, w_uq, kv_norm, w_ukv, qk_norm_q, qk_norm_k, conv_w, conv_b, conv_norm,
     w_out, ffn_norm, w_up, ffn_conv_w, ffn_conv_b, w_down) = w
    B, T, D = x.shape
    pos0 = ckv_past.shape[1]
    Tk = pos0 + T
    win, wuq, wk, wvt, one, wout, wup, wdn = _prep_weights(w_in, w_uq, w_ukv, w_out, w_up, w_down)
    inv_q, ga_q, gb_q = _rope_lane_consts(qk_norm_q)
    inv_k, ga_k, gb_k = _rope_lane_consts(qk_norm_k)
    row = lambda v: v[None].astype(F32)

    if T >= tile_rows:
        S, L, nb, nt = 1, tile_rows, B, T // tile_rows
    else:
        S, L, nb, nt = seg_per_tile, T, B // seg_per_tile, 1
    blocked = pos0 == 0 and T % attn_bq == 0
    if blocked:
        Lk, nk = attn_bk, T // attn_bk
    else:
        nk = -(-Tk // kv_rows)
        Lk = -(-Tk // (nk * LANES)) * LANES
    pad_k = nk * Lk - Tk

    x2d = x.reshape(B * T, D)
    ckv, kpe, q, c, cstate = _pre_call(
        x2d, _pad_front(conv_past, CONV_HDR), row(attn_norm), win, row(q_norm), wuq, row(kv_norm),
        inv_q, ga_q, gb_q, conv_w, row(conv_b), row(conv_norm), nb=nb, nt=nt, S=S, L=L, pos0=pos0,
        q_transposed=blocked)

    def seq(past, new, w):
        if pos0 == 0 and pad_k == 0:
            return new
        return jnp.concatenate([past, new.reshape(B, T, w), jnp.zeros((B, pad_k, w), new.dtype)],
                               axis=1).reshape(B * nk * Lk, w)

    k, vt = _kvup_call(seq(ckv_past, ckv, KV_LORA), seq(kpe_past, kpe, ROPE_DIM), wk, wvt, one,
                       inv_k, ga_k, gb_k, nb=B, nt=nk, L=Lk)

    if blocked:
        att = _attn_prompt_call(q, k, vt, B=B, T=T, bq=attn_bq, bk=attn_bk)
    else:
        att = _attn_block_call(q, k, vt, B=B, T=T, nk=nk, Lk=Lk, Tk=Tk, pos0=pos0)

    y, fstate = _post_call(x2d, att, c, _pad_front(ffn_past, FFN_HDR), wout, row(ffn_norm), wup,
                           ffn_conv_w, row(ffn_conv_b), wdn, nb=nb, nt=nt, S=S, L=L, FC=ffn_chunk)
    return (y.reshape(B, T, D), ckv.reshape(B, T, KV_LORA), kpe.reshape(B, T, ROPE_DIM),
            cstate[:, CONV_HDR - (CONV_K - 1):], fstate[:, FFN_HDR - (FFN_K - 1):])


def kernel(x_prompt, x_sample, cache_ckv, cache_kpe, state_conv, state_ffn_conv, attn_norm, w_in, q_norm, w_uq,
           kv_norm, w_ukv, qk_norm_q, qk_norm_k, conv_w, conv_b, conv_norm, w_out, ffn_norm, w_up, ffn_conv_w,
           ffn_conv_b, w_down):
    depth = w_in.shape[0]
    B = x_prompt.shape[0]
    dt = x_prompt.dtype
    d_ff = w_down.shape[1]
    conv_ch = conv_w.shape[2]
    cfg = dict(tile_rows=512, kv_rows=1024, attn_bq=512, attn_bk=512, seg_per_tile=4, ffn_chunk=1536)
    yp, ys = x_prompt, x_sample
    outs_p, outs_s = [], []
    for l in range(depth):
        w = (attn_norm[l], w_in[l], q_norm[l], w_uq[l], kv_norm[l], w_ukv[l], qk_norm_q[l], qk_norm_k[l],
             conv_w[l], conv_b[l], conv_norm[l], w_out[l], ffn_norm[l], w_up[l], ffn_conv_w[l], ffn_conv_b[l],
             w_down[l])
        yp, *rest_p = _layer(yp, jnp.zeros((B, 0, KV_LORA), dt), jnp.zeros((B, 0, ROPE_DIM), dt),
                             jnp.zeros((B, CONV_K - 1, conv_ch), dt), jnp.zeros((B, FFN_K - 1, d_ff), dt), w, **cfg)
        ys, *rest_s = _layer(ys, cache_ckv[l], cache_kpe[l], state_conv[l], state_ffn_conv[l], w, **cfg)
        outs_p.append(rest_p)
        outs_s.append(rest_s)
    stack = lambda outs, i: jnp.stack([o[i] for o in outs])
    return (yp, ys, stack(outs_p, 0), stack(outs_p, 1), stack(outs_p, 2), stack(outs_p, 3),
            stack(outs_s, 0), stack(outs_s, 1), stack(outs_s, 2), stack(outs_s, 3))
```

```python
import functools

import jax
import jax.numpy as jnp
from jax import lax
from jax.experimental import pallas as pl
from jax.experimental.pallas import tpu as pltpu

CHUNK = 64
N_HEADS = 8
QK_NOPE = 64
ROPE_DIM = 32
HEAD_DIM = QK_NOPE + ROPE_DIM
V_DIM = 64
Q_LORA = 384
KV_LORA = 256
CONV_K = 31
FFN_K = 3
ROPE_BASE = 10000.0
RMS_EPS = 1e-6
NEG_INF = -1e30
SCALE = HEAD_DIM ** -0.5
Q_SCALE = SCALE * 1.4426950408889634

LANES = 128
SUBLANES = 8
HEAD_PAIRS = N_HEADS // 2
V_ROWS = 80
CONV_HDR = 32
FFN_HDR = 8
VMEM_LIMIT = 56 * 1024 * 1024

F32 = jnp.float32
BF16 = jnp.bfloat16


def _rms(x, g):
    return x * lax.rsqrt(jnp.mean(x * x, axis=-1, keepdims=True) + RMS_EPS) * g


def _dot(a, b):
    return jnp.dot(a, b, preferred_element_type=F32)


def _init_rope_scratch(L, cos_ref, sin_ref, inv_ref):
    @pl.when((pl.program_id(0) == 0) & (pl.program_id(1) == 0))
    def _():
        ang = lax.broadcasted_iota(jnp.int32, (L, 1), 0).astype(F32) * inv_ref[...]
        cos_ref[...] = jnp.cos(ang)
        sin_ref[...] = jnp.sin(ang)


def _rope_tables(base_pos, cos_ref, sin_ref, inv_ref, ga_ref, gb_ref):
    base = base_pos.astype(F32) * inv_ref[...]
    c0, s0 = jnp.cos(base), jnp.sin(base)
    ci, si = cos_ref[...], sin_ref[...]
    return ga_ref[...] * (c0 * ci - s0 * si), gb_ref[...] * (s0 * ci + c0 * si)


def _params(sem):
    return pltpu.CompilerParams(dimension_semantics=sem, vmem_limit_bytes=VMEM_LIMIT)


def _const_spec(shape):
    nd = len(shape)
    return pl.BlockSpec(shape, lambda *_: (0,) * nd)


def _pre_kernel(x_ref, an_ref, win_ref, qn_ref, wuq_ref, kvn_ref, inv_ref, ga_ref, gb_ref,
                ckv_ref, kpe_ref, q_ref, cos_ref, sin_ref, *, S, L, pos0, q_transposed):
    t = pl.program_id(1)
    _init_rope_scratch(L, cos_ref, sin_ref, inv_ref)

    x = x_ref[...]
    h = _rms(x, an_ref[...]).astype(BF16)

    c_q = _dot(h, win_ref[:, :Q_LORA])
    hq = _rms(c_q, qn_ref[...]).astype(BF16)
    c_kv = _dot(h, win_ref[:, Q_LORA:Q_LORA + KV_LORA])
    ckv_ref[...] = _rms(c_kv, kvn_ref[...])
    o = Q_LORA + KV_LORA
    kpe_ref[...] = _dot(h, win_ref[:, o:o + LANES])[:, :ROPE_DIM]

    qa = _dot(hq, wuq_ref[:, :N_HEADS * LANES])
    qb = _dot(hq, wuq_ref[:, N_HEADS * LANES:])
    ta, tb = _rope_tables(pos0 + t * L, cos_ref, sin_ref, inv_ref, ga_ref, gb_ref)
    for s in range(S):
        rows = slice(s * L, (s + 1) * L)
        for hd in range(N_HEADS):
            cols = slice(hd * LANES, (hd + 1) * LANES)
            qh = qa[rows, cols]
            r = lax.rsqrt(jnp.sum(qh * qh, axis=-1, keepdims=True) * (1.0 / HEAD_DIM) + RMS_EPS) * Q_SCALE
            qr = (qh * ta + qb[rows, cols] * tb) * r
            if q_transposed:
                q_ref[cols, rows] = qr.T.astype(BF16)
            else:
                q_ref[rows, cols] = qr.astype(BF16)


def _pre_call(x2d, an, win, qn, wuq, kvn, inv, ga, gb, *, nb, nt, S, L, pos0, q_transposed):
    rows, d = x2d.shape
    tm = S * L
    row_blk = lambda w: pl.BlockSpec((tm, w), lambda b, t: (b * nt + t, 0))
    if q_transposed:
        q_shape, q_blk = (N_HEADS * LANES, rows), pl.BlockSpec((N_HEADS * LANES, tm), lambda b, t: (0, b * nt + t))
    else:
        q_shape, q_blk = (rows, N_HEADS * LANES), row_blk(N_HEADS * LANES)
    out_shape = (
        jax.ShapeDtypeStruct((rows, KV_LORA), F32),
        jax.ShapeDtypeStruct((rows, ROPE_DIM), F32),
        jax.ShapeDtypeStruct(q_shape, BF16),
    )
    return pl.pallas_call(
        functools.partial(_pre_kernel, S=S, L=L, pos0=pos0, q_transposed=q_transposed),
        grid=(nb, nt),
        in_specs=[row_blk(d)] + [_const_spec(a.shape) for a in (an, win, qn, wuq, kvn, inv, ga, gb)],
        out_specs=(row_blk(KV_LORA), row_blk(ROPE_DIM), q_blk),
        out_shape=out_shape,
        scratch_shapes=[pltpu.VMEM((L, LANES), F32), pltpu.VMEM((L, LANES), F32)],
        compiler_params=_params(("arbitrary", "arbitrary")),
        name="pre",
    )(x2d, an, win, qn, wuq, kvn, inv, ga, gb)


def _kvup_kernel(ckv_ref, kpe_ref, wk_ref, wvt_ref, one_ref, inv_ref, ga_ref, gb_ref, k_ref, vt_ref,
                 cos_ref, sin_ref, pe_ref, *, L):
    t = pl.program_id(1)
    _init_rope_scratch(L, cos_ref, sin_ref, inv_ref)

    @pl.when((pl.program_id(0) == 0) & (t == 0))
    def _():
        pe_ref[...] = jnp.zeros(pe_ref.shape, F32)

    ckv = ckv_ref[...].astype(BF16)
    vt = lax.dot_general(wvt_ref[...], ckv, (((1,), (1,)), ((), ())), preferred_element_type=F32)
    vt_ref[...] = (vt + one_ref[...]).astype(BF16)
    kn = _dot(ckv, wk_ref[...])
    pe_ref[:, :ROPE_DIM] = kpe_ref[...]
    x = pe_ref[...]
    half = ROPE_DIM // 2
    lane = lax.broadcasted_iota(jnp.int32, (L, LANES), 1)
    pe = pltpu.roll(x, QK_NOPE, axis=1)
    pe_swapped = jnp.where(lane < QK_NOPE + half, pltpu.roll(x, QK_NOPE - half, axis=1),
                           pltpu.roll(x, QK_NOPE + half, axis=1))
    ta, tb = _rope_tables(t * L, cos_ref, sin_ref, inv_ref, ga_ref, gb_ref)
    rot = pe_swapped * tb
    for hd in range(N_HEADS):
        cols = slice(hd * LANES, (hd + 1) * LANES)
        kh = kn[:, cols] + pe
        r = lax.rsqrt(jnp.sum(kh * kh, axis=-1, keepdims=True) * (1.0 / HEAD_DIM) + RMS_EPS)
        k_ref[:, cols] = ((kh * ta + rot) * r).astype(BF16)


def _kvup_call(ckv2d, kpe2d, wk, wvt, one, inv, ga, gb, *, nb, nt, L):
    rows = ckv2d.shape[0]
    row_blk = lambda w: pl.BlockSpec((L, w), lambda b, t: (b * nt + t, 0))
    return pl.pallas_call(
        functools.partial(_kvup_kernel, L=L),
        grid=(nb, nt),
        in_specs=[row_blk(KV_LORA), row_blk(ROPE_DIM)] + [_const_spec(a.shape) for a in (wk, wvt, one, inv, ga, gb)],
        out_specs=(row_blk(N_HEADS * LANES),
                   pl.BlockSpec((None, None, N_HEADS * V_ROWS, L), lambda b, t: (b, t, 0, 0))),
        out_shape=(jax.ShapeDtypeStruct((rows, N_HEADS * LANES), BF16),
                   jax.ShapeDtypeStruct((nb, nt, N_HEADS * V_ROWS, L), BF16)),
        scratch_shapes=[pltpu.VMEM((L, LANES), F32), pltpu.VMEM((L, LANES), F32), pltpu.VMEM((L, LANES), F32)],
        compiler_params=_params(("arbitrary", "arbitrary")),
        name="kvup",
    )(ckv2d, kpe2d, wk, wvt, one, inv, ga, gb)


def _finish_heads(accs):
    outs = [a[:V_DIM] / a[V_DIM:V_DIM + 1] for a in accs]
    return jnp.concatenate(outs, axis=0).T


def _attn_prompt_kernel(qt_ref, k_ref, vt_ref, o_ref, m_ref, acc_ref, cmax_ref, s0_ref, s1_ref, *, bq, bk,
                        unroll):
    qi = pl.program_id(2)
    half = bk // 2
    diag_blocks = bq // bk
    n_full = qi * diag_blocks
    m_ref[...] = jnp.full(m_ref.shape, NEG_INF, F32)
    acc_ref[...] = jnp.zeros(acc_ref.shape, F32)

    def allowed(k_off):
        q_chunk = lax.broadcasted_iota(jnp.int32, (half, bq), 1) // CHUNK
        k_chunk = (k_off + lax.broadcasted_iota(jnp.int32, (half, bq), 0)) // CHUNK
        return k_chunk <= q_chunk

    def scores(j, part, s_ref):
        start = pl.multiple_of(j * bk + part * half, half)
        for hh in range(2):
            cols = slice(hh * LANES, (hh + 1) * LANES)
            st = _dot(k_ref[pl.ds(start, half), cols], qt_ref[cols, :])
            s_ref[hh] = st
            cmax_ref[part, hh] = jnp.max(st, axis=0, keepdims=True)

    def update(j, part, s_ref, mask):
        for hh in range(2):
            rows = slice(hh * V_ROWS, (hh + 1) * V_ROWS)
            st = s_ref[hh]
            if mask is None:
                cmax = cmax_ref[part, hh]
            else:
                st = jnp.where(mask, st, NEG_INF)
                cmax = jnp.max(st, axis=0, keepdims=True)
            m_old = m_ref[hh]
            m_new = jnp.maximum(m_old, cmax)
            p = jnp.exp2(st - m_new).astype(BF16)
            vt = vt_ref[j, rows, part * half:(part + 1) * half]
            acc_ref[rows, :] = jnp.exp2(m_old - m_new) * acc_ref[rows, :] + _dot(vt, p)
            m_ref[hh] = m_new

    def body(j, carry):
        scores(j, 1, s1_ref)
        update(j, 0, s0_ref, None)
        scores(j + 1, 0, s0_ref)
        update(j, 1, s1_ref, None)
        return carry

    def body_unrolled(i, carry):
        for u in range(unroll):
            body(unroll * i + u, carry)
        return carry

    scores(0, 0, s0_ref)
    n_main = n_full // unroll
    lax.fori_loop(0, n_main, body_unrolled, 0)
    lax.fori_loop(n_main * unroll, n_full, body, 0)
    for d in range(diag_blocks):
        j = n_full + d
        scores(j, 1, s1_ref)
        update(j, 0, s0_ref, allowed(d * bk))
        if d + 1 < diag_blocks:
            scores(j + 1, 0, s0_ref)
        update(j, 1, s1_ref, allowed(d * bk + half))
    o_ref[...] = _finish_heads([acc_ref[hh * V_ROWS:(hh + 1) * V_ROWS, :] for hh in range(2)]).astype(BF16)


def _attn_prompt_call(qt, k2d, vt, *, B, T, bq, bk, unroll):
    nq, nk = T // bq, T // bk
    return pl.pallas_call(
        functools.partial(_attn_prompt_kernel, bq=bq, bk=bk, unroll=unroll),
        grid=(B, HEAD_PAIRS, nq),
        in_specs=[pl.BlockSpec((2 * LANES, bq), lambda b, hp, qi: (hp, b * nq + qi)),
                  pl.BlockSpec((T, 2 * LANES), lambda b, hp, qi: (b, hp)),
                  pl.BlockSpec((None, nk, 2 * V_ROWS, bk), lambda b, hp, qi: (b, 0, hp, 0))],
        out_specs=pl.BlockSpec((bq, 2 * V_DIM), lambda b, hp, qi: (b * nq + qi, hp)),
        out_shape=jax.ShapeDtypeStruct((B * T, N_HEADS * V_DIM), BF16),
        scratch_shapes=[pltpu.VMEM((2, 1, bq), F32), pltpu.VMEM((2 * V_ROWS, bq), F32),
                        pltpu.VMEM((2, 2, 1, bq), F32),
                        pltpu.VMEM((2, bk // 2, bq), F32), pltpu.VMEM((2, bk // 2, bq), F32)],
        compiler_params=_params(("arbitrary", "arbitrary", "arbitrary")),
        name="attn_prompt",
    )(qt, k2d, vt)


def _attn_block_kernel(q_ref, k_ref, vt_ref, o_ref, *, T, nk, Lk, Tk, pos0):
    Tq = -(-T // LANES) * LANES
    q_chunk = (pos0 + lax.broadcasted_iota(jnp.int32, (Lk, Tq), 1)) // CHUNK
    k_row = lax.broadcasted_iota(jnp.int32, (Lk, Tq), 0)
    accs = []
    for hh in range(2):
        cols = slice(hh * LANES, (hh + 1) * LANES)
        rows = slice(hh * V_ROWS, (hh + 1) * V_ROWS)
        q = q_ref[:, cols]
        if Tq > T:
            q = jnp.concatenate([q, jnp.zeros((Tq - T, LANES), BF16)], axis=0)
        sts = []
        for j in range(nk):
            st = lax.dot_general(k_ref[j * Lk:(j + 1) * Lk, cols], q, (((1,), (1,)), ((), ())),
                                 preferred_element_type=F32)
            k_pos = j * Lk + k_row
            sts.append(jnp.where((k_pos // CHUNK <= q_chunk) & (k_pos < Tk), st, NEG_INF))
        m = functools.reduce(jnp.maximum, [jnp.max(st, axis=0, keepdims=True) for st in sts])
        accs.append(sum(_dot(vt_ref[j, rows, :], jnp.exp2(sts[j] - m).astype(BF16)) for j in range(nk)))
    o_ref[...] = _finish_heads(accs)[:T].astype(BF16)


def _attn_block_call(q2d, k2d, vt, *, B, T, nk, Lk, Tk, pos0):
    return pl.pallas_call(
        functools.partial(_attn_block_kernel, T=T, nk=nk, Lk=Lk, Tk=Tk, pos0=pos0),
        grid=(B, HEAD_PAIRS),
        in_specs=[pl.BlockSpec((T, 2 * LANES), lambda b, hp: (b, hp)),
                  pl.BlockSpec((nk * Lk, 2 * LANES), lambda b, hp: (b, hp)),
                  pl.BlockSpec((None, nk, 2 * V_ROWS, Lk), lambda b, hp: (b, 0, hp, 0))],
        out_specs=pl.BlockSpec((T, 2 * V_DIM), lambda b, hp: (b, hp)),
        out_shape=jax.ShapeDtypeStruct((B * T, N_HEADS * V_DIM), BF16),
        compiler_params=_params(("arbitrary", "arbitrary")),
        name="attn_block",
    )(q2d, k2d, vt)


def _conv_module(h, wglu_ref, cw_ref, cb_ref, cn_ref, cstate_ref, cext_ref, conv_ref, *, S, L, RC, CB):
    conv_ch = cw_ref.shape[1]
    glu_a = _dot(h, wglu_ref[:, :conv_ch])
    glu_b = _dot(h, wglu_ref[:, conv_ch:])
    u = glu_a * jax.nn.sigmoid(glu_b)
    for s in range(S):
        cext_ref[s, CONV_HDR:, :] = u[s * L:(s + 1) * L]
    cstate_ref[...] = cext_ref[:, L:L + CONV_HDR, :]

    first = CONV_HDR - (CONV_K - 1)
    for s in range(S):
        for r0 in range(0, L, RC):
            for c0 in range(0, conv_ch, CB):
                ch = slice(c0, c0 + CB)
                y = jnp.broadcast_to(cb_ref[:, ch], (RC, CB))
                for res in range(SUBLANES):
                    taps = [k for k in range(CONV_K) if (first + k) % SUBLANES == res]
                    n_rows = RC + (SUBLANES if res else 0)
                    z = None
                    for k in taps:
                        a0 = r0 + first + k - res
                        term = cw_ref[k:k + 1, ch] * cext_ref[s, a0:a0 + n_rows, ch]
                        z = term if z is None else z + term
                    if z is not None:
                        y = y + z[res:res + RC]
                conv_ref[s * L + r0:s * L + r0 + RC, ch] = y
    y = _rms(conv_ref[...], cn_ref[...])
    return (y * jax.nn.sigmoid(y)).astype(BF16)


def _post_kernel(x_ref, att_ref, cst_ref, fst_ref, an_ref, wglu_ref, cw_ref, cb_ref, cn_ref,
                 wout_ref, fn_ref, wup_ref, fw_ref, fb_ref, wdn_ref,
                 y_ref, cstate_ref, fstate_ref, cext_ref, conv_ref, ext_ref, *, S, L, FC, RC, CB):
    t = pl.program_id(1)
    att_w = att_ref.shape[1]
    d_ff = wdn_ref.shape[0]

    @pl.when(t == 0)
    def _():
        fstate_ref[...] = fst_ref[...]
        cext_ref[:, :CONV_HDR, :] = cst_ref[...]

    @pl.when(t > 0)
    def _():
        cext_ref[:, :CONV_HDR, :] = cext_ref[:, L:L + CONV_HDR, :]

    x = x_ref[...]
    h = _rms(x, an_ref[...]).astype(BF16)
    c = _conv_module(h, wglu_ref, cw_ref, cb_ref, cn_ref, cstate_ref, cext_ref, conv_ref, S=S, L=L, RC=RC, CB=CB)
    x2 = x + _dot(att_ref[...], wout_ref[:att_w, :]) + _dot(c, wout_ref[att_w:, :])
    h2 = _rms(x2, fn_ref[...]).astype(BF16)
    y_ref[...] = x2
    for c0 in range(0, d_ff, FC):
        w = min(FC, d_ff - c0)
        cols = slice(c0, c0 + w)
        a = _dot(h2, wup_ref[:, cols])
        gate = _dot(h2, wup_ref[:, d_ff + c0:d_ff + c0 + w])
        for s in range(S):
            ext_ref[s, :FFN_HDR, :w] = fstate_ref[s, :, cols]
            ext_ref[s, FFN_HDR:, :w] = a[s * L:(s + 1) * L]
            fstate_ref[s, :, cols] = a[(s + 1) * L - FFN_HDR:(s + 1) * L]
        acts = []
        for s in range(S):
            conv = fb_ref[:, cols] + fw_ref[2:3, cols] * a[s * L:(s + 1) * L]
            for k in range(FFN_K - 1):
                off = FFN_HDR - (FFN_K - 1) + k
                conv = conv + fw_ref[k:k + 1, cols] * ext_ref[s, off:off + L, :w]
            acts.append((conv * jax.nn.sigmoid(conv) * gate[s * L:(s + 1) * L]).astype(BF16))
        act = acts[0] if S == 1 else jnp.concatenate(acts, axis=0)
        y_ref[...] += _dot(act, wdn_ref[cols, :])


def _post_call(x2d, att, cst, fst, an, wglu, cw, cb, cn, wout, fn, wup, fw, fb, wdn, *, nb, nt, S, L, FC):
    rows, d = x2d.shape
    tm = S * L
    d_ff = wdn.shape[0]
    conv_ch = cw.shape[1]
    row_blk = lambda w: pl.BlockSpec((tm, w), lambda b, t: (b * nt + t, 0))
    cstate_blk = pl.BlockSpec((S, CONV_HDR, conv_ch), lambda b, t: (b, 0, 0))
    fstate_blk = pl.BlockSpec((S, FFN_HDR, d_ff), lambda b, t: (b, 0, 0))
    consts = (an, wglu, cw, cb, cn, wout, fn, wup, fw, fb, wdn)
    return pl.pallas_call(
        functools.partial(_post_kernel, S=S, L=L, FC=FC, RC=min(L, 64), CB=min(conv_ch, 2 * LANES)),
        grid=(nb, nt),
        in_specs=[row_blk(d), row_blk(att.shape[1]), cstate_blk, fstate_blk] + [_const_spec(a.shape) for a in consts],
        out_specs=(row_blk(d), cstate_blk, fstate_blk),
        out_shape=(jax.ShapeDtypeStruct((rows, d), F32),
                   jax.ShapeDtypeStruct((nb * S, CONV_HDR, conv_ch), F32),
                   jax.ShapeDtypeStruct((nb * S, FFN_HDR, d_ff), F32)),
        scratch_shapes=[pltpu.VMEM((S, CONV_HDR + L, conv_ch), F32), pltpu.VMEM((tm, conv_ch), F32),
                        pltpu.VMEM((S, FFN_HDR + L, FC), F32)],
        compiler_params=_params(("arbitrary", "arbitrary")),
        name="post",
    )(x2d, att, cst, fst, *consts)


def _head_groups(w, n_rows):
    w = w.reshape(n_rows, N_HEADS, HEAD_DIM)
    nope, pe = w[..., :QK_NOPE], w[..., QK_NOPE:]
    pe_sw = jnp.concatenate([pe[..., ROPE_DIM // 2:], pe[..., :ROPE_DIM // 2]], axis=-1)
    pad = jnp.zeros((n_rows, N_HEADS, LANES - HEAD_DIM), w.dtype)
    straight = jnp.concatenate([nope, pe, pad], axis=-1).reshape(n_rows, N_HEADS * LANES)
    swapped = jnp.concatenate([jnp.zeros_like(nope), pe_sw, pad], axis=-1).reshape(n_rows, N_HEADS * LANES)
    return straight, swapped


def _rope_lane_consts(g):
    half = ROPE_DIM // 2
    inv = 1.0 / (ROPE_BASE ** (jnp.arange(0, ROPE_DIM, 2, dtype=F32) / ROPE_DIM))
    z = lambda n: jnp.zeros((n,), F32)
    inv_lane = jnp.concatenate([z(QK_NOPE), inv, inv, z(LANES - HEAD_DIM)])
    ga = jnp.concatenate([g, z(LANES - HEAD_DIM)])
    gb = jnp.concatenate([z(QK_NOPE), -g[QK_NOPE + half:], g[QK_NOPE:QK_NOPE + half], z(LANES - HEAD_DIM)])
    return inv_lane[None], ga[None], gb[None]


def _prep_weights(w_in, w_uq, w_ukv, w_out, w_up, w_down):
    d = w_in.shape[0]
    o = Q_LORA + KV_LORA
    pe_out = jnp.concatenate([w_in[:, o:o + ROPE_DIM], jnp.zeros((d, LANES - ROPE_DIM), w_in.dtype)], axis=-1)
    win = jnp.concatenate([w_in[:, :o], pe_out], axis=-1).astype(BF16)
    wglu = w_in[:, o + ROPE_DIM:].astype(BF16)
    wuq = jnp.concatenate(_head_groups(w_uq, Q_LORA), axis=-1).astype(BF16)
    wkv = w_ukv.reshape(KV_LORA, N_HEADS, QK_NOPE + V_DIM)
    wk = jnp.concatenate([wkv[..., :QK_NOPE], jnp.zeros((KV_LORA, N_HEADS, LANES - QK_NOPE), w_ukv.dtype)], axis=-1)
    wk = wk.reshape(KV_LORA, N_HEADS * LANES).astype(BF16)
    wvt = jnp.transpose(wkv[..., QK_NOPE:], (1, 2, 0))
    wvt = jnp.pad(wvt, ((0, 0), (0, V_ROWS - V_DIM), (0, 0))).reshape(N_HEADS * V_ROWS, KV_LORA).astype(BF16)
    one = jnp.zeros((N_HEADS, V_ROWS, 1), F32).at[:, V_DIM].set(1.0).reshape(N_HEADS * V_ROWS, 1)
    return win, wglu, wuq, wk, wvt, one, w_out.astype(BF16), w_up.astype(BF16), w_down.astype(BF16)


def _pad_front(a, n):
    return jnp.pad(a, ((0, 0), (n - a.shape[1], 0), (0, 0)))


def _layer(x, ckv_past, kpe_past, conv_past, ffn_past, w, *, tile_rows, kv_rows, attn_bq, attn_bk, attn_unroll,
           seg_per_tile, ffn_chunk):
    (attn_norm, w_in, q_norm, w_uq, kv_norm, w_ukv, qk_norm_q, qk_norm_k, conv_w, conv_b, conv_norm,
     w_out, ffn_norm, w_up, ffn_conv_w, ffn_conv_b, w_down) = w
    B, T, D = x.shape
    pos0 = ckv_past.shape[1]
    Tk = pos0 + T
    win, wglu, wuq, wk, wvt, one, wout, wup, wdn = _prep_weights(w_in, w_uq, w_ukv, w_out, w_up, w_down)
    inv_q, ga_q, gb_q = _rope_lane_consts(qk_norm_q)
    inv_k, ga_k, gb_k = _rope_lane_consts(qk_norm_k)
    row = lambda v: v[None].astype(F32)

    if T >= tile_rows:
        S, L, nb, nt = 1, tile_rows, B, T // tile_rows
    else:
        S, L, nb, nt = seg_per_tile, T, B // seg_per_tile, 1
    blocked = pos0 == 0 and T % attn_bq == 0
    if blocked:
        Lk, nk = attn_bk, T // attn_bk
    else:
        nk = -(-Tk // kv_rows)
        Lk = -(-Tk // (nk * LANES)) * LANES
    pad_k = nk * Lk - Tk

    x2d = x.reshape(B * T, D)
    ckv, kpe, q = _pre_call(x2d, row(attn_norm), win, row(q_norm), wuq, row(kv_norm), inv_q, ga_q, gb_q,
                            nb=nb, nt=nt, S=S, L=L, pos0=pos0, q_transposed=blocked)

    def seq(past, new, w):
        if pos0 == 0 and pad_k == 0:
            return new
        return jnp.concatenate([past, new.reshape(B, T, w), jnp.zeros((B, pad_k, w), new.dtype)],
                               axis=1).reshape(B * nk * Lk, w)

    k, vt = _kvup_call(seq(ckv_past, ckv, KV_LORA), seq(kpe_past, kpe, ROPE_DIM), wk, wvt, one,
                       inv_k, ga_k, gb_k, nb=B, nt=nk, L=Lk)

    if blocked:
        att = _attn_prompt_call(q, k, vt, B=B, T=T, bq=attn_bq, bk=attn_bk, unroll=attn_unroll)
    else:
        att = _attn_block_call(q, k, vt, B=B, T=T, nk=nk, Lk=Lk, Tk=Tk, pos0=pos0)

    y, cstate, fstate = _post_call(
        x2d, att, _pad_front(conv_past, CONV_HDR), _pad_front(ffn_past, FFN_HDR), row(attn_norm), wglu,
        conv_w, row(conv_b), row(conv_norm), wout, row(ffn_norm), wup, ffn_conv_w, row(ffn_conv_b), wdn,
        nb=nb, nt=nt, S=S, L=L, FC=ffn_chunk)
    return (y.reshape(B, T, D), ckv.reshape(B, T, KV_LORA), kpe.reshape(B, T, ROPE_DIM),
            cstate[:, CONV_HDR - (CONV_K - 1):], fstate[:, FFN_HDR - (FFN_K - 1):])


def kernel(x_prompt, x_sample, cache_ckv, cache_kpe, state_conv, state_ffn_conv, attn_norm, w_in, q_norm, w_uq,
           kv_norm, w_ukv, qk_norm_q, qk_norm_k, conv_w, conv_b, conv_norm, w_out, ffn_norm, w_up, ffn_conv_w,
           ffn_conv_b, w_down):
    depth = w_in.shape[0]
    B = x_prompt.shape[0]
    dt = x_prompt.dtype
    d_ff = w_down.shape[1]
    conv_ch = conv_w.shape[2]
    cfg = dict(tile_rows=512, kv_rows=1024, attn_bq=512, attn_bk=512, attn_unroll=4, seg_per_tile=4, ffn_chunk=1536)
    yp, ys = x_prompt, x_sample
    outs_p, outs_s = [], []
    for l in range(depth):
        w = (attn_norm[l], w_in[l], q_norm[l], w_uq[l], kv_norm[l], w_ukv[l], qk_norm_q[l], qk_norm_k[l],
             conv_w[l], conv_b[l], conv_norm[l], w_out[l], ffn_norm[l], w_up[l], ffn_conv_w[l], ffn_conv_b[l],
             w_down[l])
        yp, *rest_p = _layer(yp, jnp.zeros((B, 0, KV_LORA), dt), jnp.zeros((B, 0, ROPE_DIM), dt),
                             jnp.zeros((B, CONV_K - 1, conv_ch), dt), jnp.zeros((B, FFN_K - 1, d_ff), dt), w, **cfg)
        ys, *rest_s = _layer(ys, cache_ckv[l], cache_kpe[l], state_conv[l], state_ffn_conv[l], w, **cfg)
        outs_p.append(rest_p)
        outs_s.append(rest_s)
    stack = lambda outs, i: jnp.stack([o[i] for o in outs])
    return (yp, ys, stack(outs_p, 0), stack(outs_p, 1), stack(outs_p, 2), stack(outs_p, 3),
            stack(outs_s, 0), stack(outs_s, 1), stack(outs_s, 2), stack(outs_s, 3))
```

```python
import functools

import jax
import jax.numpy as jnp
from jax import lax
from jax.experimental import pallas as pl
from jax.experimental.pallas import tpu as pltpu

CHUNK = 64
N_HEADS = 8
QK_NOPE = 64
ROPE_DIM = 32
HEAD_DIM = QK_NOPE + ROPE_DIM
V_DIM = 64
Q_LORA = 384
KV_LORA = 256
CONV_K = 31
FFN_K = 3
ROPE_BASE = 10000.0
RMS_EPS = 1e-6
NEG_INF = -1e30
SCALE = HEAD_DIM ** -0.5
Q_SCALE = SCALE * 1.4426950408889634

LANES = 128
SUBLANES = 8
HEAD_PAIRS = N_HEADS // 2
V_ROWS = 80
CONV_HDR = 32
FFN_HDR = 8
VMEM_LIMIT = 56 * 1024 * 1024

F32 = jnp.float32
BF16 = jnp.bfloat16


def _rms(x, g):
    return x * lax.rsqrt(jnp.mean(x * x, axis=-1, keepdims=True) + RMS_EPS) * g


def _dot(a, b):
    return jnp.dot(a, b, preferred_element_type=F32)


def _init_rope_scratch(L, cos_ref, sin_ref, inv_ref):
    @pl.when((pl.program_id(0) == 0) & (pl.program_id(1) == 0))
    def _():
        ang = lax.broadcasted_iota(jnp.int32, (L, 1), 0).astype(F32) * inv_ref[...]
        cos_ref[...] = jnp.cos(ang)
        sin_ref[...] = jnp.sin(ang)


def _rope_tables(base_pos, cos_ref, sin_ref, inv_ref, ga_ref, gb_ref):
    base = base_pos.astype(F32) * inv_ref[...]
    c0, s0 = jnp.cos(base), jnp.sin(base)
    ci, si = cos_ref[...], sin_ref[...]
    return ga_ref[...] * (c0 * ci - s0 * si), gb_ref[...] * (s0 * ci + c0 * si)


def _params(sem):
    return pltpu.CompilerParams(dimension_semantics=sem, vmem_limit_bytes=VMEM_LIMIT)


def _const_spec(shape):
    nd = len(shape)
    return pl.BlockSpec(shape, lambda *_: (0,) * nd)


def _pre_kernel(x_ref, an_ref, win_ref, qn_ref, wuq_ref, kvn_ref, inv_ref, ga_ref, gb_ref,
                ckv_ref, kpe_ref, q_ref, cos_ref, sin_ref, *, S, L, pos0, q_transposed):
    t = pl.program_id(1)
    _init_rope_scratch(L, cos_ref, sin_ref, inv_ref)

    x = x_ref[...]
    h = _rms(x, an_ref[...]).astype(BF16)

    c_q = _dot(h, win_ref[:, :Q_LORA])
    hq = _rms(c_q, qn_ref[...]).astype(BF16)
    c_kv = _dot(h, win_ref[:, Q_LORA:Q_LORA + KV_LORA])
    ckv_ref[...] = _rms(c_kv, kvn_ref[...])
    o = Q_LORA + KV_LORA
    kpe_ref[...] = _dot(h, win_ref[:, o:o + LANES])[:, :ROPE_DIM]

    qa = _dot(hq, wuq_ref[:, :N_HEADS * LANES])
    qb = _dot(hq, wuq_ref[:, N_HEADS * LANES:])
    ta, tb = _rope_tables(pos0 + t * L, cos_ref, sin_ref, inv_ref, ga_ref, gb_ref)
    for s in range(S):
        rows = slice(s * L, (s + 1) * L)
        for hd in range(N_HEADS):
            cols = slice(hd * LANES, (hd + 1) * LANES)
            qh = qa[rows, cols]
            r = lax.rsqrt(jnp.sum(qh * qh, axis=-1, keepdims=True) * (1.0 / HEAD_DIM) + RMS_EPS) * Q_SCALE
            qr = (qh * ta + qb[rows, cols] * tb) * r
            if q_transposed:
                q_ref[cols, rows] = qr.T.astype(BF16)
            else:
                q_ref[rows, cols] = qr.astype(BF16)


def _pre_call(x2d, an, win, qn, wuq, kvn, inv, ga, gb, *, nb, nt, S, L, pos0, q_transposed):
    rows, d = x2d.shape
    tm = S * L
    row_blk = lambda w: pl.BlockSpec((tm, w), lambda b, t: (b * nt + t, 0))
    if q_transposed:
        q_shape, q_blk = (N_HEADS * LANES, rows), pl.BlockSpec((N_HEADS * LANES, tm), lambda b, t: (0, b * nt + t))
    else:
        q_shape, q_blk = (rows, N_HEADS * LANES), row_blk(N_HEADS * LANES)
    out_shape = (
        jax.ShapeDtypeStruct((rows, KV_LORA), F32),
        jax.ShapeDtypeStruct((rows, ROPE_DIM), F32),
        jax.ShapeDtypeStruct(q_shape, BF16),
    )
    return pl.pallas_call(
        functools.partial(_pre_kernel, S=S, L=L, pos0=pos0, q_transposed=q_transposed),
        grid=(nb, nt),
        in_specs=[row_blk(d)] + [_const_spec(a.shape) for a in (an, win, qn, wuq, kvn, inv, ga, gb)],
        out_specs=(row_blk(KV_LORA), row_blk(ROPE_DIM), q_blk),
        out_shape=out_shape,
        scratch_shapes=[pltpu.VMEM((L, LANES), F32), pltpu.VMEM((L, LANES), F32)],
        compiler_params=_params(("arbitrary", "arbitrary")),
        name="pre",
    )(x2d, an, win, qn, wuq, kvn, inv, ga, gb)


def _kvup_kernel(ckv_ref, kpe_ref, wk_ref, wvt_ref, one_ref, inv_ref, ga_ref, gb_ref, k_ref, vt_ref,
                 cos_ref, sin_ref, pe_ref, *, L):
    t = pl.program_id(1)
    _init_rope_scratch(L, cos_ref, sin_ref, inv_ref)

    @pl.when((pl.program_id(0) == 0) & (t == 0))
    def _():
        pe_ref[...] = jnp.zeros(pe_ref.shape, F32)

    ckv = ckv_ref[...].astype(BF16)
    vt = lax.dot_general(wvt_ref[...], ckv, (((1,), (1,)), ((), ())), preferred_element_type=F32)
    vt_ref[...] = (vt + one_ref[...]).astype(BF16)
    kn = _dot(ckv, wk_ref[...])
    pe_ref[:, :ROPE_DIM] = kpe_ref[...]
    x = pe_ref[...]
    half = ROPE_DIM // 2
    lane = lax.broadcasted_iota(jnp.int32, (L, LANES), 1)
    pe = pltpu.roll(x, QK_NOPE, axis=1)
    pe_swapped = jnp.where(lane < QK_NOPE + half, pltpu.roll(x, QK_NOPE - half, axis=1),
                           pltpu.roll(x, QK_NOPE + half, axis=1))
    ta, tb = _rope_tables(t * L, cos_ref, sin_ref, inv_ref, ga_ref, gb_ref)
    rot = pe_swapped * tb
    for hd in range(N_HEADS):
        cols = slice(hd * LANES, (hd + 1) * LANES)
        kh = kn[:, cols] + pe
        r = lax.rsqrt(jnp.sum(kh * kh, axis=-1, keepdims=True) * (1.0 / HEAD_DIM) + RMS_EPS)
        k_ref[:, cols] = ((kh * ta + rot) * r).astype(BF16)


def _kvup_call(ckv2d, kpe2d, wk, wvt, one, inv, ga, gb, *, nb, nt, L):
    rows = ckv2d.shape[0]
    row_blk = lambda w: pl.BlockSpec((L, w), lambda b, t: (b * nt + t, 0))
    return pl.pallas_call(
        functools.partial(_kvup_kernel, L=L),
        grid=(nb, nt),
        in_specs=[row_blk(KV_LORA), row_blk(ROPE_DIM)] + [_const_spec(a.shape) for a in (wk, wvt, one, inv, ga, gb)],
        out_specs=(row_blk(N_HEADS * LANES),
                   pl.BlockSpec((None, None, N_HEADS * V_ROWS, L), lambda b, t: (b, t, 0, 0))),
        out_shape=(jax.ShapeDtypeStruct((rows, N_HEADS * LANES), BF16),
                   jax.ShapeDtypeStruct((nb, nt, N_HEADS * V_ROWS, L), BF16)),
        scratch_shapes=[pltpu.VMEM((L, LANES), F32), pltpu.VMEM((L, LANES), F32), pltpu.VMEM((L, LANES), F32)],
        compiler_params=_params(("arbitrary", "arbitrary")),
        name="kvup",
    )(ckv2d, kpe2d, wk, wvt, one, inv, ga, gb)


def _finish_heads(accs):
    outs = [a[:V_DIM] / a[V_DIM:V_DIM + 1] for a in accs]
    return jnp.concatenate(outs, axis=0).T


def _attn_prompt_kernel(qt_ref, k_ref, vt_ref, o_ref, m_ref, acc_ref, cmax_ref, s0_ref, s1_ref, *, bq, bk,
                        unroll):
    qi = pl.program_id(2)
    half = bk // 2
    diag_blocks = bq // bk
    n_full = qi * diag_blocks
    m_ref[...] = jnp.full(m_ref.shape, NEG_INF, F32)
    acc_ref[...] = jnp.zeros(acc_ref.shape, F32)

    def allowed(k_off):
        q_chunk = lax.broadcasted_iota(jnp.int32, (half, bq), 1) // CHUNK
        k_chunk = (k_off + lax.broadcasted_iota(jnp.int32, (half, bq), 0)) // CHUNK
        return k_chunk <= q_chunk

    def scores(j, part, s_ref):
        start = pl.multiple_of(j * bk + part * half, half)
        for hh in range(2):
            cols = slice(hh * LANES, (hh + 1) * LANES)
            st = _dot(k_ref[pl.ds(start, half), cols], qt_ref[cols, :])
            s_ref[hh] = st
            cmax_ref[part, hh] = jnp.max(st, axis=0, keepdims=True)

    def update(j, part, s_ref, mask):
        for hh in range(2):
            rows = slice(hh * V_ROWS, (hh + 1) * V_ROWS)
            st = s_ref[hh]
            if mask is None:
                cmax = cmax_ref[part, hh]
            else:
                st = jnp.where(mask, st, NEG_INF)
                cmax = jnp.max(st, axis=0, keepdims=True)
            m_old = m_ref[hh]
            m_new = jnp.maximum(m_old, cmax)
            p = jnp.exp2(st - m_new).astype(BF16)
            vt = vt_ref[j, rows, part * half:(part + 1) * half]
            acc_ref[rows, :] = jnp.exp2(m_old - m_new) * acc_ref[rows, :] + _dot(vt, p)
            m_ref[hh] = m_new

    def body(j, carry):
        scores(j, 1, s1_ref)
        update(j, 0, s0_ref, None)
        scores(j + 1, 0, s0_ref)
        update(j, 1, s1_ref, None)
        return carry

    def body_unrolled(i, carry):
        for u in range(unroll):
            body(unroll * i + u, carry)
        return carry

    scores(0, 0, s0_ref)
    n_main = n_full // unroll
    lax.fori_loop(0, n_main, body_unrolled, 0)
    lax.fori_loop(n_main * unroll, n_full, body, 0)
    for d in range(diag_blocks):
        j = n_full + d
        scores(j, 1, s1_ref)
        update(j, 0, s0_ref, allowed(d * bk))
        if d + 1 < diag_blocks:
            scores(j + 1, 0, s0_ref)
        update(j, 1, s1_ref, allowed(d * bk + half))
    o_ref[...] = _finish_heads([acc_ref[hh * V_ROWS:(hh + 1) * V_ROWS, :] for hh in range(2)]).astype(BF16)


def _attn_prompt_call(qt, k2d, vt, *, B, T, bq, bk, unroll):
    nq, nk = T // bq, T // bk
    return pl.pallas_call(
        functools.partial(_attn_prompt_kernel, bq=bq, bk=bk, unroll=unroll),
        grid=(B, HEAD_PAIRS, nq),
        in_specs=[pl.BlockSpec((2 * LANES, bq), lambda b, hp, qi: (hp, b * nq + qi)),
                  pl.BlockSpec((T, 2 * LANES), lambda b, hp, qi: (b, hp)),
                  pl.BlockSpec((None, nk, 2 * V_ROWS, bk), lambda b, hp, qi: (b, 0, hp, 0))],
        out_specs=pl.BlockSpec((bq, 2 * V_DIM), lambda b, hp, qi: (b * nq + qi, hp)),
        out_shape=jax.ShapeDtypeStruct((B * T, N_HEADS * V_DIM), BF16),
        scratch_shapes=[pltpu.VMEM((2, 1, bq), F32), pltpu.VMEM((2 * V_ROWS, bq), F32),
                        pltpu.VMEM((2, 2, 1, bq), F32),
                        pltpu.VMEM((2, bk // 2, bq), F32), pltpu.VMEM((2, bk // 2, bq), F32)],
        compiler_params=_params(("arbitrary", "arbitrary", "arbitrary")),
        name="attn_prompt",
    )(qt, k2d, vt)


def _attn_block_kernel(q_ref, k_ref, vt_ref, o_ref, *, T, nk, Lk, Tk, pos0):
    Tq = -(-T // LANES) * LANES
    q_chunk = (pos0 + lax.broadcasted_iota(jnp.int32, (Lk, Tq), 1)) // CHUNK
    k_row = lax.broadcasted_iota(jnp.int32, (Lk, Tq), 0)
    accs = []
    for hh in range(2):
        cols = slice(hh * LANES, (hh + 1) * LANES)
        rows = slice(hh * V_ROWS, (hh + 1) * V_ROWS)
        q = q_ref[:, cols]
        if Tq > T:
            q = jnp.concatenate([q, jnp.zeros((Tq - T, LANES), BF16)], axis=0)
        sts = []
        for j in range(nk):
            st = lax.dot_general(k_ref[j * Lk:(j + 1) * Lk, cols], q, (((1,), (1,)), ((), ())),
                                 preferred_element_type=F32)
            k_pos = j * Lk + k_row
            sts.append(jnp.where((k_pos // CHUNK <= q_chunk) & (k_pos < Tk), st, NEG_INF))
        m = functools.reduce(jnp.maximum, [jnp.max(st, axis=0, keepdims=True) for st in sts])
        accs.append(sum(_dot(vt_ref[j, rows, :], jnp.exp2(sts[j] - m).astype(BF16)) for j in range(nk)))
    o_ref[...] = _finish_heads(accs)[:T].astype(BF16)


def _attn_block_call(q2d, k2d, vt, *, B, T, nk, Lk, Tk, pos0):
    return pl.pallas_call(
        functools.partial(_attn_block_kernel, T=T, nk=nk, Lk=Lk, Tk=Tk, pos0=pos0),
        grid=(B, HEAD_PAIRS),
        in_specs=[pl.BlockSpec((T, 2 * LANES), lambda b, hp: (b, hp)),
                  pl.BlockSpec((nk * Lk, 2 * LANES), lambda b, hp: (b, hp)),
                  pl.BlockSpec((None, nk, 2 * V_ROWS, Lk), lambda b, hp: (b, 0, hp, 0))],
        out_specs=pl.BlockSpec((T, 2 * V_DIM), lambda b, hp: (b, hp)),
        out_shape=jax.ShapeDtypeStruct((B * T, N_HEADS * V_DIM), BF16),
        compiler_params=_params(("arbitrary", "arbitrary")),
        name="attn_block",
    )(q2d, k2d, vt)


def _conv_module(h, wglu_ref, cw_ref, cb_ref, cn_ref, cstate_ref, cext_ref, conv_ref, *, S, L, RC, CB):
    conv_ch = cw_ref.shape[1]
    glu_a = _dot(h, wglu_ref[:, :conv_ch])
    glu_b = _dot(h, wglu_ref[:, conv_ch:])
    u = glu_a * jax.nn.sigmoid(glu_b)
    for s in range(S):
        cext_ref[s, CONV_HDR:, :] = u[s * L:(s + 1) * L]
    cstate_ref[...] = cext_ref[:, L:L + CONV_HDR, :]

    first = CONV_HDR - (CONV_K - 1)

    def piece(s, r0, c0, after=None):
        ch = slice(c0, c0 + CB)
        bias = cb_ref[:, ch]
        if after is not None:
            bits = lax.bitcast_convert_type(after[:1, :CB], jnp.uint32)
            bias = bias + lax.bitcast_convert_type((bits >> 16) >> 16, F32)
        y = jnp.broadcast_to(bias, (RC, CB))
        for res in range(SUBLANES):
            taps = [k for k in range(CONV_K) if (first + k) % SUBLANES == res]
            n_rows = RC + (SUBLANES if res else 0)
            z = None
            for k in taps:
                a0 = r0 + first + k - res
                term = cw_ref[k:k + 1, ch] * cext_ref[s, a0:a0 + n_rows, ch]
                z = term if z is None else z + term
            if z is not None:
                y = y + z[res:res + RC]
        conv_ref[s * L + r0:s * L + r0 + RC, ch] = y

    def finish():
        y = _rms(conv_ref[...], cn_ref[...])
        return (y * jax.nn.sigmoid(y)).astype(BF16)

    pieces = [functools.partial(piece, s, r0, c0)
              for s in range(S) for r0 in range(0, L, RC) for c0 in range(0, conv_ch, CB)]
    return pieces, finish


def _post_kernel(x_ref, att_ref, cst_ref, fst_ref, an_ref, wglu_ref, cw_ref, cb_ref, cn_ref,
                 wout_ref, fn_ref, wup_ref, fw_ref, fb_ref, wdn_ref,
                 y_ref, cstate_ref, fstate_ref, cext_ref, conv_ref, ext_ref, *hold_refs, S, L, FC, RC, CB, n_tiles,
                 pipelined):
    t = pl.program_id(1)
    att_w = att_ref.shape[1]
    d_ff = wdn_ref.shape[0]

    @pl.when(t == 0)
    def _():
        fstate_ref[...] = fst_ref[...]
        cext_ref[:, :CONV_HDR, :] = cst_ref[...]
        for r in hold_refs:
            r[...] = jnp.zeros(r.shape, r.dtype)

    @pl.when(t > 0)
    def _():
        cext_ref[:, :CONV_HDR, :] = cext_ref[:, L:L + CONV_HDR, :]

    if pipelined:
        x_hold, c_hold = hold_refs
        x_ffn, c_ffn = x_hold[...], c_hold[...]
        conv_live, ffn_live = t < n_tiles, t > 0

    x = x_ref[...]
    h = _rms(x, an_ref[...]).astype(BF16)
    old_cstate = cstate_ref[...] if pipelined else None
    pieces, conv_finish = _conv_module(h, wglu_ref, cw_ref, cb_ref, cn_ref, cstate_ref, cext_ref, conv_ref,
                                       S=S, L=L, RC=RC, CB=CB)
    n_slots = 1 + 3 * -(-d_ff // FC)
    per_slot = -(-len(pieces) // n_slots) if pipelined else len(pieces)

    def conv_slot(after=None):
        for _ in range(min(per_slot, len(pieces))):
            pieces.pop(0)(after=after)

    if pipelined:
        cstate_ref[...] = jnp.where(conv_live, cstate_ref[...], old_cstate)
        x_hold[...] = x
    else:
        conv_slot()
        x_ffn, c_ffn = x, conv_finish()

    x2 = x_ffn + _dot(att_ref[...], wout_ref[:att_w, :]) + _dot(c_ffn, wout_ref[att_w:, :])
    conv_slot(x2)
    h2 = _rms(x2, fn_ref[...]).astype(BF16)
    y_ref[...] = x2
    for c0 in range(0, d_ff, FC):
        w = min(FC, d_ff - c0)
        cols = slice(c0, c0 + w)
        a = _dot(h2, wup_ref[:, cols])
        conv_slot(a)
        gate = _dot(h2, wup_ref[:, d_ff + c0:d_ff + c0 + w])
        conv_slot(gate)
        for s in range(S):
            hist = fstate_ref[s, :, cols]
            tail = a[(s + 1) * L - FFN_HDR:(s + 1) * L]
            ext_ref[s, :FFN_HDR, :w] = hist
            ext_ref[s, FFN_HDR:, :w] = a[s * L:(s + 1) * L]
            fstate_ref[s, :, cols] = jnp.where(ffn_live, tail, hist) if pipelined else tail
        acts = []
        for s in range(S):
            conv = fb_ref[:, cols] + fw_ref[2:3, cols] * a[s * L:(s + 1) * L]
            for k in range(FFN_K - 1):
                off = FFN_HDR - (FFN_K - 1) + k
                conv = conv + fw_ref[k:k + 1, cols] * ext_ref[s, off:off + L, :w]
            acts.append((conv * jax.nn.sigmoid(conv) * gate[s * L:(s + 1) * L]).astype(BF16))
        act = acts[0] if S == 1 else jnp.concatenate(acts, axis=0)
        down = _dot(act, wdn_ref[cols, :])
        y_ref[...] += down
        conv_slot(down)
    if pipelined:
        c_hold[...] = conv_finish()


def _post_call(x2d, att, cst, fst, an, wglu, cw, cb, cn, wout, fn, wup, fw, fb, wdn, *, nb, nt, S, L, FC):
    rows, d = x2d.shape
    tm = S * L
    d_ff = wdn.shape[0]
    conv_ch = cw.shape[1]
    pipelined = nt > 1
    if pipelined:
        steps = nt + 1
        conv_blk = lambda w: pl.BlockSpec((tm, w), lambda b, t: (b * nt + jnp.minimum(t, nt - 1), 0))
        ffn_blk = lambda w: pl.BlockSpec((tm, w), lambda b, t: (b * nt + jnp.maximum(t - 1, 0), 0))
        hold = [pltpu.VMEM((tm, d), F32), pltpu.VMEM((tm, conv_ch), BF16)]
    else:
        steps = nt
        conv_blk = ffn_blk = lambda w: pl.BlockSpec((tm, w), lambda b, t: (b * nt + t, 0))
        hold = []
    cstate_blk = pl.BlockSpec((S, CONV_HDR, conv_ch), lambda b, t: (b, 0, 0))
    fstate_blk = pl.BlockSpec((S, FFN_HDR, d_ff), lambda b, t: (b, 0, 0))
    consts = (an, wglu, cw, cb, cn, wout, fn, wup, fw, fb, wdn)
    return pl.pallas_call(
        functools.partial(_post_kernel, S=S, L=L, FC=FC, RC=min(L, 64), CB=min(conv_ch, 2 * LANES), n_tiles=nt,
                          pipelined=pipelined),
        grid=(nb, steps),
        in_specs=[conv_blk(d), ffn_blk(att.shape[1]), cstate_blk, fstate_blk] + [_const_spec(a.shape) for a in consts],
        out_specs=(ffn_blk(d), cstate_blk, fstate_blk),
        out_shape=(jax.ShapeDtypeStruct((rows, d), F32),
                   jax.ShapeDtypeStruct((nb * S, CONV_HDR, conv_ch), F32),
                   jax.ShapeDtypeStruct((nb * S, FFN_HDR, d_ff), F32)),
        scratch_shapes=[pltpu.VMEM((S, CONV_HDR + L, conv_ch), F32), pltpu.VMEM((tm, conv_ch), F32),
                        pltpu.VMEM((S, FFN_HDR + L, FC), F32)] + hold,
        compiler_params=_params(("arbitrary", "arbitrary")),
        name="post",
    )(x2d, att, cst, fst, *consts)


def _head_groups(w, n_rows):
    w = w.reshape(n_rows, N_HEADS, HEAD_DIM)
    nope, pe = w[..., :QK_NOPE], w[..., QK_NOPE:]
    pe_sw = jnp.concatenate([pe[..., ROPE_DIM // 2:], pe[..., :ROPE_DIM // 2]], axis=-1)
    pad = jnp.zeros((n_rows, N_HEADS, LANES - HEAD_DIM), w.dtype)
    straight = jnp.concatenate([nope, pe, pad], axis=-1).reshape(n_rows, N_HEADS * LANES)
    swapped = jnp.concatenate([jnp.zeros_like(nope), pe_sw, pad], axis=-1).reshape(n_rows, N_HEADS * LANES)
    return straight, swapped


def _rope_lane_consts(g):
    half = ROPE_DIM // 2
    inv = 1.0 / (ROPE_BASE ** (jnp.arange(0, ROPE_DIM, 2, dtype=F32) / ROPE_DIM))
    z = lambda n: jnp.zeros((n,), F32)
    inv_lane = jnp.concatenate([z(QK_NOPE), inv, inv, z(LANES - HEAD_DIM)])
    ga = jnp.concatenate([g, z(LANES - HEAD_DIM)])
    gb = jnp.concatenate([z(QK_NOPE), -g[QK_NOPE + half:], g[QK_NOPE:QK_NOPE + half], z(LANES - HEAD_DIM)])
    return inv_lane[None], ga[None], gb[None]


def _prep_weights(w_in, w_uq, w_ukv, w_out, w_up, w_down):
    d = w_in.shape[0]
    o = Q_LORA + KV_LORA
    pe_out = jnp.concatenate([w_in[:, o:o + ROPE_DIM], jnp.zeros((d, LANES - ROPE_DIM), w_in.dtype)], axis=-1)
    win = jnp.concatenate([w_in[:, :o], pe_out], axis=-1).astype(BF16)
    wglu = w_in[:, o + ROPE_DIM:].astype(BF16)
    wuq = jnp.concatenate(_head_groups(w_uq, Q_LORA), axis=-1).astype(BF16)
    wkv = w_ukv.reshape(KV_LORA, N_HEADS, QK_NOPE + V_DIM)
    wk = jnp.concatenate([wkv[..., :QK_NOPE], jnp.zeros((KV_LORA, N_HEADS, LANES - QK_NOPE), w_ukv.dtype)], axis=-1)
    wk = wk.reshape(KV_LORA, N_HEADS * LANES).astype(BF16)
    wvt = jnp.transpose(wkv[..., QK_NOPE:], (1, 2, 0))
    wvt = jnp.pad(wvt, ((0, 0), (0, V_ROWS - V_DIM), (0, 0))).reshape(N_HEADS * V_ROWS, KV_LORA).astype(BF16)
    one = jnp.zeros((N_HEADS, V_ROWS, 1), F32).at[:, V_DIM].set(1.0).reshape(N_HEADS * V_ROWS, 1)
    return win, wglu, wuq, wk, wvt, one, w_out.astype(BF16), w_up.astype(BF16), w_down.astype(BF16)


def _pad_front(a, n):
    return jnp.pad(a, ((0, 0), (n - a.shape[1], 0), (0, 0)))


def _layer(x, ckv_past, kpe_past, conv_past, ffn_past, w, *, tile_rows, kv_rows, attn_bq, attn_bk, attn_unroll,
           seg_per_tile, ffn_chunk):
    (attn_norm, w_in, q_norm, w_uq, kv_norm, w_ukv, qk_norm_q, qk_norm_k, conv_w, conv_b, conv_norm,
     w_out, ffn_norm, w_up, ffn_conv_w, ffn_conv_b, w_down) = w
    B, T, D = x.shape
    pos0 = ckv_past.shape[1]
    Tk = pos0 + T
    win, wglu, wuq, wk, wvt, one, wout, wup, wdn = _prep_weights(w_in, w_uq, w_ukv, w_out, w_up, w_down)
    inv_q, ga_q, gb_q = _rope_lane_consts(qk_norm_q)
    inv_k, ga_k, gb_k = _rope_lane_consts(qk_norm_k)
    row = lambda v: v[None].astype(F32)

    if T >= tile_rows:
        S, L, nb, nt = 1, tile_rows, B, T // tile_rows
    else:
        S, L, nb, nt = seg_per_tile, T, B // seg_per_tile, 1
    blocked = pos0 == 0 and T % attn_bq == 0
    if blocked:
        Lk, nk = attn_bk, T // attn_bk
    else:
        nk = -(-Tk // kv_rows)
        Lk = -(-Tk // (nk * LANES)) * LANES
    pad_k = nk * Lk - Tk

    x2d = x.reshape(B * T, D)
    ckv, kpe, q = _pre_call(x2d, row(attn_norm), win, row(q_norm), wuq, row(kv_norm), inv_q, ga_q, gb_q,
                            nb=nb, nt=nt, S=S, L=L, pos0=pos0, q_transposed=blocked)

    def seq(past, new, w):
        if pos0 == 0 and pad_k == 0:
            return new
        return jnp.concatenate([past, new.reshape(B, T, w), jnp.zeros((B, pad_k, w), new.dtype)],
                               axis=1).reshape(B * nk * Lk, w)

    k, vt = _kvup_call(seq(ckv_past, ckv, KV_LORA), seq(kpe_past, kpe, ROPE_DIM), wk, wvt, one,
                       inv_k, ga_k, gb_k, nb=B, nt=nk, L=Lk)

    if blocked:
        att = _attn_prompt_call(q, k, vt, B=B, T=T, bq=attn_bq, bk=attn_bk, unroll=attn_unroll)
    else:
        att = _attn_block_call(q, k, vt, B=B, T=T, nk=nk, Lk=Lk, Tk=Tk, pos0=pos0)

    y, cstate, fstate = _post_call(
        x2d, att, _pad_front(conv_past, CONV_HDR), _pad_front(ffn_past, FFN_HDR), row(attn_norm), wglu,
        conv_w, row(conv_b), row(conv_norm), wout, row(ffn_norm), wup, ffn_conv_w, row(ffn_conv_b), wdn,
        nb=nb, nt=nt, S=S, L=L, FC=ffn_chunk)
    return (y.reshape(B, T, D), ckv.reshape(B, T, KV_LORA), kpe.reshape(B, T, ROPE_DIM),
            cstate[:, CONV_HDR - (CONV_K - 1):], fstate[:, FFN_HDR - (FFN_K - 1):])


def kernel(x_prompt, x_sample, cache_ckv, cache_kpe, state_conv, state_ffn_conv, attn_norm, w_in, q_norm, w_uq,
           kv_norm, w_ukv, qk_norm_q, qk_norm_k, conv_w, conv_b, conv_norm, w_out, ffn_norm, w_up, ffn_conv_w,
           ffn_conv_b, w_down):
    depth = w_in.shape[0]
    B = x_prompt.shape[0]
    dt = x_prompt.dtype
    d_ff = w_down.shape[1]
    conv_ch = conv_w.shape[2]
    cfg = dict(tile_rows=512, kv_rows=1024, attn_bq=512, attn_bk=512, attn_unroll=4, seg_per_tile=4, ffn_chunk=1536)
    yp, ys = x_prompt, x_sample
    outs_p, outs_s = [], []
    for l in range(depth):
        w = (attn_norm[l], w_in[l], q_norm[l], w_uq[l], kv_norm[l], w_ukv[l], qk_norm_q[l], qk_norm_k[l],
             conv_w[l], conv_b[l], conv_norm[l], w_out[l], ffn_norm[l], w_up[l], ffn_conv_w[l], ffn_conv_b[l],
             w_down[l])
        yp, *rest_p = _layer(yp, jnp.zeros((B, 0, KV_LORA), dt), jnp.zeros((B, 0, ROPE_DIM), dt),
                             jnp.zeros((B, CONV_K - 1, conv_ch), dt), jnp.zeros((B, FFN_K - 1, d_ff), dt), w, **cfg)
        ys, *rest_s = _layer(ys, cache_ckv[l], cache_kpe[l], state_conv[l], state_ffn_conv[l], w, **cfg)
        outs_p.append(rest_p)
        outs_s.append(rest_s)
    stack = lambda outs, i: jnp.stack([o[i] for o in outs])
    return (yp, ys, stack(outs_p, 0), stack(outs_p, 1), stack(outs_p, 2), stack(outs_p, 3),
            stack(outs_s, 0), stack(outs_s, 1), stack(outs_s, 2), stack(outs_s, 3))
```

```python
import functools

import jax
import jax.numpy as jnp
from jax import lax
from jax.experimental import pallas as pl
from jax.experimental.pallas import tpu as pltpu

CHUNK = 64
N_HEADS = 8
QK_NOPE = 64
ROPE_DIM = 32
HEAD_DIM = QK_NOPE + ROPE_DIM
V_DIM = 64
Q_LORA = 384
KV_LORA = 256
CONV_K = 31
FFN_K = 3
ROPE_BASE = 10000.0
RMS_EPS = 1e-6
NEG_INF = -1e30
SCALE = HEAD_DIM ** -0.5
Q_SCALE = SCALE * 1.4426950408889634

LANES = 128
SUBLANES = 8
HEAD_PAIRS = N_HEADS // 2
V_ROWS = 80
CONV_HDR = 32
FFN_HDR = 8
VMEM_LIMIT = 56 * 1024 * 1024

F32 = jnp.float32
BF16 = jnp.bfloat16


def _rms(x, g):
    return x * lax.rsqrt(jnp.mean(x * x, axis=-1, keepdims=True) + RMS_EPS) * g


def _dot(a, b):
    return jnp.dot(a, b, preferred_element_type=F32)


def _init_rope_scratch(L, cos_ref, sin_ref, inv_ref):
    @pl.when((pl.program_id(0) == 0) & (pl.program_id(1) == 0))
    def _():
        ang = lax.broadcasted_iota(jnp.int32, (L, 1), 0).astype(F32) * inv_ref[...]
        cos_ref[...] = jnp.cos(ang)
        sin_ref[...] = jnp.sin(ang)


def _rope_tables(base_pos, cos_ref, sin_ref, inv_ref, ga_ref, gb_ref):
    base = base_pos.astype(F32) * inv_ref[...]
    c0, s0 = jnp.cos(base), jnp.sin(base)
    ci, si = cos_ref[...], sin_ref[...]
    return ga_ref[...] * (c0 * ci - s0 * si), gb_ref[...] * (s0 * ci + c0 * si)


def _params(sem):
    return pltpu.CompilerParams(dimension_semantics=sem, vmem_limit_bytes=VMEM_LIMIT)


def _const_spec(shape):
    nd = len(shape)
    return pl.BlockSpec(shape, lambda *_: (0,) * nd)


def _pre_kernel(x_ref, an_ref, win_ref, qn_ref, wuq_ref, kvn_ref, inv_ref, ga_ref, gb_ref,
                ckv_ref, kpe_ref, q_ref, cos_ref, sin_ref, *, S, L, pos0, q_transposed):
    t = pl.program_id(1)
    _init_rope_scratch(L, cos_ref, sin_ref, inv_ref)

    x = x_ref[...]
    h = _rms(x, an_ref[...]).astype(BF16)

    c_q = _dot(h, win_ref[:, :Q_LORA])
    hq = _rms(c_q, qn_ref[...]).astype(BF16)
    c_kv = _dot(h, win_ref[:, Q_LORA:Q_LORA + KV_LORA])
    ckv_ref[...] = _rms(c_kv, kvn_ref[...])
    o = Q_LORA + KV_LORA
    kpe_ref[...] = _dot(h, win_ref[:, o:o + LANES])[:, :ROPE_DIM]

    qa = _dot(hq, wuq_ref[:, :N_HEADS * LANES])
    qb = _dot(hq, wuq_ref[:, N_HEADS * LANES:])
    ta, tb = _rope_tables(pos0 + t * L, cos_ref, sin_ref, inv_ref, ga_ref, gb_ref)
    for s in range(S):
        rows = slice(s * L, (s + 1) * L)
        for hd in range(N_HEADS):
            cols = slice(hd * LANES, (hd + 1) * LANES)
            qh = qa[rows, cols]
            r = lax.rsqrt(jnp.sum(qh * qh, axis=-1, keepdims=True) * (1.0 / HEAD_DIM) + RMS_EPS) * Q_SCALE
            qr = (qh * ta + qb[rows, cols] * tb) * r
            if q_transposed:
                q_ref[cols, rows] = qr.T.astype(BF16)
            else:
                q_ref[rows, cols] = qr.astype(BF16)


def _pre_call(x2d, an, win, qn, wuq, kvn, inv, ga, gb, *, nb, nt, S, L, pos0, q_transposed):
    rows, d = x2d.shape
    tm = S * L
    row_blk = lambda w: pl.BlockSpec((tm, w), lambda b, t: (b * nt + t, 0))
    if q_transposed:
        q_shape, q_blk = (N_HEADS * LANES, rows), pl.BlockSpec((N_HEADS * LANES, tm), lambda b, t: (0, b * nt + t))
    else:
        q_shape, q_blk = (rows, N_HEADS * LANES), row_blk(N_HEADS * LANES)
    out_shape = (
        jax.ShapeDtypeStruct((rows, KV_LORA), F32),
        jax.ShapeDtypeStruct((rows, ROPE_DIM), F32),
        jax.ShapeDtypeStruct(q_shape, BF16),
    )
    return pl.pallas_call(
        functools.partial(_pre_kernel, S=S, L=L, pos0=pos0, q_transposed=q_transposed),
        grid=(nb, nt),
        in_specs=[row_blk(d)] + [_const_spec(a.shape) for a in (an, win, qn, wuq, kvn, inv, ga, gb)],
        out_specs=(row_blk(KV_LORA), row_blk(ROPE_DIM), q_blk),
        out_shape=out_shape,
        scratch_shapes=[pltpu.VMEM((L, LANES), F32), pltpu.VMEM((L, LANES), F32)],
        compiler_params=_params(("arbitrary", "arbitrary")),
        name="pre",
    )(x2d, an, win, qn, wuq, kvn, inv, ga, gb)


def _kvup_kernel(ckv_ref, kpe_ref, wk_ref, wvt_ref, one_ref, inv_ref, ga_ref, gb_ref, k_ref, vt_ref,
                 cos_ref, sin_ref, pe_ref, *, L):
    t = pl.program_id(1)
    _init_rope_scratch(L, cos_ref, sin_ref, inv_ref)

    @pl.when((pl.program_id(0) == 0) & (t == 0))
    def _():
        pe_ref[...] = jnp.zeros(pe_ref.shape, F32)

    ckv = ckv_ref[...].astype(BF16)
    vt = lax.dot_general(wvt_ref[...], ckv, (((1,), (1,)), ((), ())), preferred_element_type=F32)
    vt_ref[...] = (vt + one_ref[...]).astype(BF16)
    kn = _dot(ckv, wk_ref[...])
    pe_ref[:, :ROPE_DIM] = kpe_ref[...]
    x = pe_ref[...]
    half = ROPE_DIM // 2
    lane = lax.broadcasted_iota(jnp.int32, (L, LANES), 1)
    pe = pltpu.roll(x, QK_NOPE, axis=1)
    pe_swapped = jnp.where(lane < QK_NOPE + half, pltpu.roll(x, QK_NOPE - half, axis=1),
                           pltpu.roll(x, QK_NOPE + half, axis=1))
    ta, tb = _rope_tables(t * L, cos_ref, sin_ref, inv_ref, ga_ref, gb_ref)
    rot = pe_swapped * tb
    for hd in range(N_HEADS):
        cols = slice(hd * LANES, (hd + 1) * LANES)
        kh = kn[:, cols] + pe
        r = lax.rsqrt(jnp.sum(kh * kh, axis=-1, keepdims=True) * (1.0 / HEAD_DIM) + RMS_EPS)
        k_ref[:, cols] = ((kh * ta + rot) * r).astype(BF16)


def _kvup_call(ckv2d, kpe2d, wk, wvt, one, inv, ga, gb, *, nb, nt, L):
    rows = ckv2d.shape[0]
    row_blk = lambda w: pl.BlockSpec((L, w), lambda b, t: (b * nt + t, 0))
    return pl.pallas_call(
        functools.partial(_kvup_kernel, L=L),
        grid=(nb, nt),
        in_specs=[row_blk(KV_LORA), row_blk(ROPE_DIM)] + [_const_spec(a.shape) for a in (wk, wvt, one, inv, ga, gb)],
        out_specs=(row_blk(N_HEADS * LANES),
                   pl.BlockSpec((None, None, N_HEADS * V_ROWS, L), lambda b, t: (b, t, 0, 0))),
        out_shape=(jax.ShapeDtypeStruct((rows, N_HEADS * LANES), BF16),
                   jax.ShapeDtypeStruct((nb, nt, N_HEADS * V_ROWS, L), BF16)),
        scratch_shapes=[pltpu.VMEM((L, LANES), F32), pltpu.VMEM((L, LANES), F32), pltpu.VMEM((L, LANES), F32)],
        compiler_params=_params(("arbitrary", "arbitrary")),
        name="kvup",
    )(ckv2d, kpe2d, wk, wvt, one, inv, ga, gb)


def _finish_heads(accs):
    outs = [a[:V_DIM] / a[V_DIM:V_DIM + 1] for a in accs]
    return jnp.concatenate(outs, axis=0).T


def _attn_prompt_kernel(qt_ref, k_ref, vt_ref, o_ref, m_ref, acc_ref, cmax_ref, s0_ref, s1_ref, *, bq, bk,
                        unroll):
    qi = pl.program_id(2)
    half = bk // 2
    diag_blocks = bq // bk
    n_full = qi * diag_blocks
    m_ref[...] = jnp.full(m_ref.shape, NEG_INF, F32)
    acc_ref[...] = jnp.zeros(acc_ref.shape, F32)

    def allowed(k_off):
        q_chunk = lax.broadcasted_iota(jnp.int32, (half, bq), 1) // CHUNK
        k_chunk = (k_off + lax.broadcasted_iota(jnp.int32, (half, bq), 0)) // CHUNK
        return k_chunk <= q_chunk

    def scores(j, part, s_ref, q0=0):
        start = pl.multiple_of(j * bk + part * half, half)
        for hh in range(2):
            cols = slice(hh * LANES, (hh + 1) * LANES)
            st = _dot(k_ref[pl.ds(start, half), cols], qt_ref[cols, q0:])
            s_ref[hh, :, q0:] = st
            cmax_ref[part, hh, :, q0:] = jnp.max(st, axis=0, keepdims=True)

    def update(j, part, s_ref, mask, q0=0):
        for hh in range(2):
            rows = slice(hh * V_ROWS, (hh + 1) * V_ROWS)
            st = s_ref[hh, :, q0:]
            if mask is None:
                cmax = cmax_ref[part, hh, :, q0:]
            else:
                st = jnp.where(mask[:, q0:], st, NEG_INF)
                cmax = jnp.max(st, axis=0, keepdims=True)
            m_old = m_ref[hh, :, q0:]
            m_new = jnp.maximum(m_old, cmax)
            p = jnp.exp2(st - m_new).astype(BF16)
            vt = vt_ref[j, rows, part * half:(part + 1) * half]
            acc_ref[rows, q0:] = jnp.exp2(m_old - m_new) * acc_ref[rows, q0:] + _dot(vt, p)
            m_ref[hh, :, q0:] = m_new

    def body(j, carry):
        scores(j, 1, s1_ref)
        update(j, 0, s0_ref, None)
        scores(j + 1, 0, s0_ref)
        update(j, 1, s1_ref, None)
        return carry

    def body_unrolled(i, carry):
        for u in range(unroll):
            body(unroll * i + u, carry)
        return carry

    scores(0, 0, s0_ref)
    n_main = n_full // unroll
    lax.fori_loop(0, n_main, body_unrolled, 0)
    lax.fori_loop(n_main * unroll, n_full, body, 0)
    first_col = lambda k_off: k_off // LANES * LANES
    for d in range(diag_blocks):
        j = n_full + d
        scores(j, 1, s1_ref, first_col(d * bk + half))
        update(j, 0, s0_ref, allowed(d * bk), first_col(d * bk))
        if d + 1 < diag_blocks:
            scores(j + 1, 0, s0_ref, first_col((d + 1) * bk))
        update(j, 1, s1_ref, allowed(d * bk + half), first_col(d * bk + half))
    o_ref[...] = _finish_heads([acc_ref[hh * V_ROWS:(hh + 1) * V_ROWS, :] for hh in range(2)]).astype(BF16)


def _attn_prompt_call(qt, k2d, vt, *, B, T, bq, bk, unroll):
    nq, nk = T // bq, T // bk
    return pl.pallas_call(
        functools.partial(_attn_prompt_kernel, bq=bq, bk=bk, unroll=unroll),
        grid=(B, HEAD_PAIRS, nq),
        in_specs=[pl.BlockSpec((2 * LANES, bq), lambda b, hp, qi: (hp, b * nq + qi)),
                  pl.BlockSpec((T, 2 * LANES), lambda b, hp, qi: (b, hp)),
                  pl.BlockSpec((None, nk, 2 * V_ROWS, bk), lambda b, hp, qi: (b, 0, hp, 0))],
        out_specs=pl.BlockSpec((bq, 2 * V_DIM), lambda b, hp, qi: (b * nq + qi, hp)),
        out_shape=jax.ShapeDtypeStruct((B * T, N_HEADS * V_DIM), BF16),
        scratch_shapes=[pltpu.VMEM((2, 1, bq), F32), pltpu.VMEM((2 * V_ROWS, bq), F32),
                        pltpu.VMEM((2, 2, 1, bq), F32),
                        pltpu.VMEM((2, bk // 2, bq), F32), pltpu.VMEM((2, bk // 2, bq), F32)],
        compiler_params=_params(("arbitrary", "arbitrary", "arbitrary")),
        name="attn_prompt",
    )(qt, k2d, vt)


def _attn_block_kernel(q_ref, k_ref, vt_ref, o_ref, *, T, nk, Lk, Tk, pos0):
    Tq = -(-T // LANES) * LANES
    q_chunk = (pos0 + lax.broadcasted_iota(jnp.int32, (Lk, Tq), 1)) // CHUNK
    k_row = lax.broadcasted_iota(jnp.int32, (Lk, Tq), 0)
    accs = []
    for hh in range(2):
        cols = slice(hh * LANES, (hh + 1) * LANES)
        rows = slice(hh * V_ROWS, (hh + 1) * V_ROWS)
        q = q_ref[:, cols]
        if Tq > T:
            q = jnp.concatenate([q, jnp.zeros((Tq - T, LANES), BF16)], axis=0)
        sts = []
        for j in range(nk):
            st = lax.dot_general(k_ref[j * Lk:(j + 1) * Lk, cols], q, (((1,), (1,)), ((), ())),
                                 preferred_element_type=F32)
            k_pos = j * Lk + k_row
            sts.append(jnp.where((k_pos // CHUNK <= q_chunk) & (k_pos < Tk), st, NEG_INF))
        m = functools.reduce(jnp.maximum, [jnp.max(st, axis=0, keepdims=True) for st in sts])
        accs.append(sum(_dot(vt_ref[j, rows, :], jnp.exp2(sts[j] - m).astype(BF16)) for j in range(nk)))
    o_ref[...] = _finish_heads(accs)[:T].astype(BF16)


def _attn_block_call(q2d, k2d, vt, *, B, T, nk, Lk, Tk, pos0):
    return pl.pallas_call(
        functools.partial(_attn_block_kernel, T=T, nk=nk, Lk=Lk, Tk=Tk, pos0=pos0),
        grid=(B, HEAD_PAIRS),
        in_specs=[pl.BlockSpec((T, 2 * LANES), lambda b, hp: (b, hp)),
                  pl.BlockSpec((nk * Lk, 2 * LANES), lambda b, hp: (b, hp)),
                  pl.BlockSpec((None, nk, 2 * V_ROWS, Lk), lambda b, hp: (b, 0, hp, 0))],
        out_specs=pl.BlockSpec((T, 2 * V_DIM), lambda b, hp: (b, hp)),
        out_shape=jax.ShapeDtypeStruct((B * T, N_HEADS * V_DIM), BF16),
        compiler_params=_params(("arbitrary", "arbitrary")),
        name="attn_block",
    )(q2d, k2d, vt)


def _conv_module(h, wglu_ref, cw_ref, cb_ref, cn_ref, cstate_ref, cext_ref, conv_ref, *, S, L, RC, CB):
    conv_ch = cw_ref.shape[1]
    glu_a = _dot(h, wglu_ref[:, :conv_ch])
    glu_b = _dot(h, wglu_ref[:, conv_ch:])
    u = glu_a * jax.nn.sigmoid(glu_b)
    for s in range(S):
        cext_ref[s, CONV_HDR:, :] = u[s * L:(s + 1) * L]
    cstate_ref[...] = cext_ref[:, L:L + CONV_HDR, :]

    first = CONV_HDR - (CONV_K - 1)

    def piece(s, r0, c0, after=None):
        ch = slice(c0, c0 + CB)
        bias = cb_ref[:, ch]
        if after is not None:
            bits = lax.bitcast_convert_type(after[:1, :CB], jnp.uint32)
            bias = bias + lax.bitcast_convert_type((bits >> 16) >> 16, F32)
        y = jnp.broadcast_to(bias, (RC, CB))
        for res in range(SUBLANES):
            taps = [k for k in range(CONV_K) if (first + k) % SUBLANES == res]
            n_rows = RC + (SUBLANES if res else 0)
            z = None
            for k in taps:
                a0 = r0 + first + k - res
                term = cw_ref[k:k + 1, ch] * cext_ref[s, a0:a0 + n_rows, ch]
                z = term if z is None else z + term
            if z is not None:
                y = y + z[res:res + RC]
        conv_ref[s * L + r0:s * L + r0 + RC, ch] = y

    def finish():
        y = _rms(conv_ref[...], cn_ref[...])
        return (y * jax.nn.sigmoid(y)).astype(BF16)

    pieces = [functools.partial(piece, s, r0, c0)
              for s in range(S) for r0 in range(0, L, RC) for c0 in range(0, conv_ch, CB)]
    return pieces, finish


def _post_kernel(x_ref, att_ref, cst_ref, fst_ref, an_ref, wglu_ref, cw_ref, cb_ref, cn_ref,
                 wout_ref, fn_ref, wup_ref, fw_ref, fb_ref, wdn_ref,
                 y_ref, cstate_ref, fstate_ref, cext_ref, conv_ref, ext_ref, *hold_refs, S, L, FC, RC, CB, n_tiles,
                 pipelined):
    t = pl.program_id(1)
    att_w = att_ref.shape[1]
    d_ff = wdn_ref.shape[0]

    @pl.when(t == 0)
    def _():
        fstate_ref[...] = fst_ref[...]
        cstate_ref[...] = cst_ref[...]
        cext_ref[:, :CONV_HDR, :] = cst_ref[...]
        for r in hold_refs:
            r[...] = jnp.zeros(r.shape, r.dtype)

    @pl.when(t > 0)
    def _():
        cext_ref[:, :CONV_HDR, :] = cext_ref[:, L:L + CONV_HDR, :]

    if pipelined:
        x_hold, c_hold = hold_refs
        x_ffn, c_ffn = x_hold[...], c_hold[...]
        conv_live, ffn_live = t < n_tiles, t > 0

    x = x_ref[...]
    h = _rms(x, an_ref[...]).astype(BF16)
    old_cstate = cstate_ref[...] if pipelined else None
    pieces, conv_finish = _conv_module(h, wglu_ref, cw_ref, cb_ref, cn_ref, cstate_ref, cext_ref, conv_ref,
                                       S=S, L=L, RC=RC, CB=CB)
    n_slots = 1 + 3 * -(-d_ff // FC)
    per_slot = -(-len(pieces) // n_slots) if pipelined else len(pieces)

    def conv_slot(after=None):
        for _ in range(min(per_slot, len(pieces))):
            pieces.pop(0)(after=after)

    if pipelined:
        cstate_ref[...] = jnp.where(conv_live, cstate_ref[...], old_cstate)
        x_hold[...] = x
    else:
        conv_slot()
        x_ffn, c_ffn = x, conv_finish()

    x2 = x_ffn + _dot(att_ref[...], wout_ref[:att_w, :]) + _dot(c_ffn, wout_ref[att_w:, :])
    conv_slot(x2)
    h2 = _rms(x2, fn_ref[...]).astype(BF16)
    y_ref[...] = x2
    for c0 in range(0, d_ff, FC):
        w = min(FC, d_ff - c0)
        cols = slice(c0, c0 + w)
        a = _dot(h2, wup_ref[:, cols])
        conv_slot(a)
        gate = _dot(h2, wup_ref[:, d_ff + c0:d_ff + c0 + w])
        conv_slot(gate)
        for s in range(S):
            hist = fstate_ref[s, :, cols]
            tail = a[(s + 1) * L - FFN_HDR:(s + 1) * L]
            ext_ref[s, :FFN_HDR, :w] = hist
            ext_ref[s, FFN_HDR:, :w] = a[s * L:(s + 1) * L]
            fstate_ref[s, :, cols] = jnp.where(ffn_live, tail, hist) if pipelined else tail
        acts = []
        for s in range(S):
            conv = fb_ref[:, cols] + fw_ref[2:3, cols] * a[s * L:(s + 1) * L]
            for k in range(FFN_K - 1):
                off = FFN_HDR - (FFN_K - 1) + k
                conv = conv + fw_ref[k:k + 1, cols] * ext_ref[s, off:off + L, :w]
            acts.append((conv * jax.nn.sigmoid(conv) * gate[s * L:(s + 1) * L]).astype(BF16))
        act = acts[0] if S == 1 else jnp.concatenate(acts, axis=0)
        down = _dot(act, wdn_ref[cols, :])
        y_ref[...] += down
        conv_slot(down)
    if pipelined:
        c_hold[...] = conv_finish()


def _post_call(x2d, att, cst, fst, an, wglu, cw, cb, cn, wout, fn, wup, fw, fb, wdn, *, nb, nt, S, L, FC):
    rows, d = x2d.shape
    tm = S * L
    d_ff = wdn.shape[0]
    conv_ch = cw.shape[1]
    pipelined = nt > 1
    if pipelined:
        steps = nt + 1
        conv_blk = lambda w: pl.BlockSpec((tm, w), lambda b, t: (b * nt + jnp.minimum(t, nt - 1), 0))
        ffn_blk = lambda w: pl.BlockSpec((tm, w), lambda b, t: (b * nt + jnp.maximum(t - 1, 0), 0))
        hold = [pltpu.VMEM((tm, d), F32), pltpu.VMEM((tm, conv_ch), BF16)]
    else:
        steps = nt
        conv_blk = ffn_blk = lambda w: pl.BlockSpec((tm, w), lambda b, t: (b * nt + t, 0))
        hold = []
    cstate_blk = pl.BlockSpec((S, CONV_HDR, conv_ch), lambda b, t: (b, 0, 0))
    fstate_blk = pl.BlockSpec((S, FFN_HDR, d_ff), lambda b, t: (b, 0, 0))
    consts = (an, wglu, cw, cb, cn, wout, fn, wup, fw, fb, wdn)
    return pl.pallas_call(
        functools.partial(_post_kernel, S=S, L=L, FC=FC, RC=min(L, 64), CB=min(conv_ch, 2 * LANES), n_tiles=nt,
                          pipelined=pipelined),
        grid=(nb, steps),
        in_specs=[conv_blk(d), ffn_blk(att.shape[1]), cstate_blk, fstate_blk] + [_const_spec(a.shape) for a in consts],
        out_specs=(ffn_blk(d), cstate_blk, fstate_blk),
        out_shape=(jax.ShapeDtypeStruct((rows, d), F32),
                   jax.ShapeDtypeStruct((nb * S, CONV_HDR, conv_ch), F32),
                   jax.ShapeDtypeStruct((nb * S, FFN_HDR, d_ff), F32)),
        scratch_shapes=[pltpu.VMEM((S, CONV_HDR + L, conv_ch), F32), pltpu.VMEM((tm, conv_ch), F32),
                        pltpu.VMEM((S, FFN_HDR + L, FC), F32)] + hold,
        compiler_params=_params(("arbitrary", "arbitrary")),
        name="post",
    )(x2d, att, cst, fst, *consts)


def _head_groups(w, n_rows):
    w = w.reshape(n_rows, N_HEADS, HEAD_DIM)
    nope, pe = w[..., :QK_NOPE], w[..., QK_NOPE:]
    pe_sw = jnp.concatenate([pe[..., ROPE_DIM // 2:], pe[..., :ROPE_DIM // 2]], axis=-1)
    pad = jnp.zeros((n_rows, N_HEADS, LANES - HEAD_DIM), w.dtype)
    straight = jnp.concatenate([nope, pe, pad], axis=-1).reshape(n_rows, N_HEADS * LANES)
    swapped = jnp.concatenate([jnp.zeros_like(nope), pe_sw, pad], axis=-1).reshape(n_rows, N_HEADS * LANES)
    return straight, swapped


def _rope_lane_consts(g):
    half = ROPE_DIM // 2
    inv = 1.0 / (ROPE_BASE ** (jnp.arange(0, ROPE_DIM, 2, dtype=F32) / ROPE_DIM))
    z = lambda n: jnp.zeros((n,), F32)
    inv_lane = jnp.concatenate([z(QK_NOPE), inv, inv, z(LANES - HEAD_DIM)])
    ga = jnp.concatenate([g, z(LANES - HEAD_DIM)])
    gb = jnp.concatenate([z(QK_NOPE), -g[QK_NOPE + half:], g[QK_NOPE:QK_NOPE + half], z(LANES - HEAD_DIM)])
    return inv_lane[None], ga[None], gb[None]


def _prep_weights(w_in, w_uq, w_ukv, w_out, w_up, w_down):
    d = w_in.shape[0]
    o = Q_LORA + KV_LORA
    pe_out = jnp.concatenate([w_in[:, o:o + ROPE_DIM], jnp.zeros((d, LANES - ROPE_DIM), w_in.dtype)], axis=-1)
    win = jnp.concatenate([w_in[:, :o], pe_out], axis=-1).astype(BF16)
    wglu = w_in[:, o + ROPE_DIM:].astype(BF16)
    wuq = jnp.concatenate(_head_groups(w_uq, Q_LORA), axis=-1).astype(BF16)
    wkv = w_ukv.reshape(KV_LORA, N_HEADS, QK_NOPE + V_DIM)
    wk = jnp.concatenate([wkv[..., :QK_NOPE], jnp.zeros((KV_LORA, N_HEADS, LANES - QK_NOPE), w_ukv.dtype)], axis=-1)
    wk = wk.reshape(KV_LORA, N_HEADS * LANES).astype(BF16)
    wvt = jnp.transpose(wkv[..., QK_NOPE:], (1, 2, 0))
    wvt = jnp.pad(wvt, ((0, 0), (0, V_ROWS - V_DIM), (0, 0))).reshape(N_HEADS * V_ROWS, KV_LORA).astype(BF16)
    one = jnp.zeros((N_HEADS, V_ROWS, 1), F32).at[:, V_DIM].set(1.0).reshape(N_HEADS * V_ROWS, 1)
    return win, wglu, wuq, wk, wvt, one, w_out.astype(BF16), w_up.astype(BF16), w_down.astype(BF16)


def _pad_front(a, n):
    return jnp.pad(a, ((0, 0), (n - a.shape[1], 0), (0, 0)))


def _layer(x, ckv_past, kpe_past, conv_past, ffn_past, w, *, tile_rows, kv_rows, attn_bq, attn_bk, attn_unroll,
           seg_per_tile, ffn_chunk):
    (attn_norm, w_in, q_norm, w_uq, kv_norm, w_ukv, qk_norm_q, qk_norm_k, conv_w, conv_b, conv_norm,
     w_out, ffn_norm, w_up, ffn_conv_w, ffn_conv_b, w_down) = w
    B, T, D = x.shape
    pos0 = ckv_past.shape[1]
    Tk = pos0 + T
    win, wglu, wuq, wk, wvt, one, wout, wup, wdn = _prep_weights(w_in, w_uq, w_ukv, w_out, w_up, w_down)
    inv_q, ga_q, gb_q = _rope_lane_consts(qk_norm_q)
    inv_k, ga_k, gb_k = _rope_lane_consts(qk_norm_k)
    row = lambda v: v[None].astype(F32)

    if T >= tile_rows:
        S, L, nb, nt = 1, tile_rows, B, T // tile_rows
    else:
        S, L, nb, nt = seg_per_tile, T, B // seg_per_tile, 1
    blocked = pos0 == 0 and T % attn_bq == 0
    if blocked:
        Lk, nk = attn_bk, T // attn_bk
    else:
        nk = -(-Tk // kv_rows)
        Lk = -(-Tk // (nk * LANES)) * LANES
    pad_k = nk * Lk - Tk

    x2d = x.reshape(B * T, D)
    ckv, kpe, q = _pre_call(x2d, row(attn_norm), win, row(q_norm), wuq, row(kv_norm), inv_q, ga_q, gb_q,
                            nb=nb, nt=nt, S=S, L=L, pos0=pos0, q_transposed=blocked)

    def seq(past, new, w):
        if pos0 == 0 and pad_k == 0:
            return new
        return jnp.concatenate([past, new.reshape(B, T, w), jnp.zeros((B, pad_k, w), new.dtype)],
                               axis=1).reshape(B * nk * Lk, w)

    k, vt = _kvup_call(seq(ckv_past, ckv, KV_LORA), seq(kpe_past, kpe, ROPE_DIM), wk, wvt, one,
                       inv_k, ga_k, gb_k, nb=B, nt=nk, L=Lk)

    if blocked:
        att = _attn_prompt_call(q, k, vt, B=B, T=T, bq=attn_bq, bk=attn_bk, unroll=attn_unroll)
    else:
        att = _attn_block_call(q, k, vt, B=B, T=T, nk=nk, Lk=Lk, Tk=Tk, pos0=pos0)

    y, cstate, fstate = _post_call(
        x2d, att, _pad_front(conv_past, CONV_HDR), _pad_front(ffn_past, FFN_HDR), row(attn_norm), wglu,
        conv_w, row(conv_b), row(conv_norm), wout, row(ffn_norm), wup, ffn_conv_w, row(ffn_conv_b), wdn,
        nb=nb, nt=nt, S=S, L=L, FC=ffn_chunk)
    return (y.reshape(B, T, D), ckv.reshape(B, T, KV_LORA), kpe.reshape(B, T, ROPE_DIM),
            cstate[:, CONV_HDR - (CONV_K - 1):], fstate[:, FFN_HDR - (FFN_K - 1):])


def kernel(x_prompt, x_sample, cache_ckv, cache_kpe, state_conv, state_ffn_conv, attn_norm, w_in, q_norm, w_uq,
           kv_norm, w_ukv, qk_norm_q, qk_norm_k, conv_w, conv_b, conv_norm, w_out, ffn_norm, w_up, ffn_conv_w,
           ffn_conv_b, w_down):
    depth = w_in.shape[0]
    B = x_prompt.shape[0]
    dt = x_prompt.dtype
    d_ff = w_down.shape[1]
    conv_ch = conv_w.shape[2]
    cfg = dict(tile_rows=512, kv_rows=1024, attn_bq=512, attn_bk=512, attn_unroll=4, seg_per_tile=4, ffn_chunk=1536)
    yp, ys = x_prompt, x_sample
    outs_p, outs_s = [], []
    for l in range(depth):
        w = (attn_norm[l], w_in[l], q_norm[l], w_uq[l], kv_norm[l], w_ukv[l], qk_norm_q[l], qk_norm_k[l],
             conv_w[l], conv_b[l], conv_norm[l], w_out[l], ffn_norm[l], w_up[l], ffn_conv_w[l], ffn_conv_b[l],
             w_down[l])
        yp, *rest_p = _layer(yp, jnp.zeros((B, 0, KV_LORA), dt), jnp.zeros((B, 0, ROPE_DIM), dt),
                             jnp.zeros((B, CONV_K - 1, conv_ch), dt), jnp.zeros((B, FFN_K - 1, d_ff), dt), w, **cfg)
        ys, *rest_s = _layer(ys, cache_ckv[l], cache_kpe[l], state_conv[l], state_ffn_conv[l], w, **cfg)
        outs_p.append(rest_p)
        outs_s.append(rest_s)
    stack = lambda outs, i: jnp.stack([o[i] for o in outs])
    return (yp, ys, stack(outs_p, 0), stack(outs_p, 1), stack(outs_p, 2), stack(outs_p, 3),
            stack(outs_s, 0), stack(outs_s, 1), stack(outs_s, 2), stack(outs_s, 3))
```
